```python
import math
import jax
import jax.numpy as jnp
from jax import lax
import numpy as np

D_MODEL = 2048
BATCH = 2
SEQ = 4096
DEPTH = 4
DEC_BATCH = 8
DEC_SEQ = 8
PAST_LEN = 16384
PAGE_SIZE = 128

N_MIXERS = 4
EPS = 1e-6
A_HEADS = 8
A_DK = 128
A_DV = 256
A_CHUNK = 64
A_IN = 2 * A_HEADS * A_DK + 2 * A_HEADS * A_DV + 2 * A_HEADS
B_HEADS = 16
B_DH = 128
Q_BLOCK = 128
C_GROUPS = ((128, 1), (512, 4), (2048, 16))
C_NG = 3
C_HPG = 8
C_DH = 128
C_HEADS = C_NG * C_HPG
N_BUCKETS = 32
MAX_DISTANCE = 2048
D_QK_HEADS = 16
D_V_HEADS = 32
D_DK = 128
D_DV = 128
D_CONV = 4
D_CHUNK = 64
D_CONV_CH = 2 * D_QK_HEADS * D_DK + D_V_HEADS * D_DV
D_IN = D_CONV_CH + D_V_HEADS * D_DV + 2 * D_V_HEADS
D_FF = 5632
N_EXPERTS = 8
TOP_K = 2
D_FF_EXPERT = 7168
N_DENSE = (DEPTH + 1) // 2
N_MOE = DEPTH // 2

kernel_name = 'hybrid_mlstm_stickbreak_dilated_gdn_decode_step'


def rmsnorm(x, g):
    xf = x.astype(jnp.float32)
    y = xf * lax.rsqrt(jnp.mean(xf * xf, axis=-1, keepdims=True) + EPS)
    return (y * g.astype(jnp.float32)).astype(x.dtype)


def l2norm(x):
    return x * lax.rsqrt(jnp.sum(x * x, axis=-1, keepdims=True) + EPS)


def mlstm_mixer(h, C0, n0, m0, w_in, b_if, g_norm, w_out):
    f32 = jnp.float32
    Bn, T, _ = h.shape
    H, DK, DV = A_HEADS, A_DK, A_DV
    proj = h @ w_in
    cuts = np.cumsum([H * DK, H * DK, H * DV, H * DV, H]).tolist()
    q, k, v, o, gi, gf = jnp.split(proj, cuts, axis=-1)
    q = q.reshape(Bn, T, H, DK).astype(f32) * DK ** -0.5
    k = k.reshape(Bn, T, H, DK).astype(f32)
    v = v.reshape(Bn, T, H, DV).astype(f32)
    log_i = gi.astype(f32) + b_if[0].astype(f32)
    log_f = jax.nn.log_sigmoid(gf.astype(f32) + b_if[1].astype(f32))
    L = math.gcd(T, A_CHUNK)
    nC = T // L
    chunk = lambda a: jnp.moveaxis(a.reshape(Bn, nC, L, *a.shape[2:]), 1, 0)
    causal = jnp.tril(jnp.ones((L, L), dtype=bool))[None, :, :, None]

    def step(carry, xs):
        C, n, m = carry
        qc, kc, vc, lic, lfc = xs
        b = jnp.cumsum(lfc, axis=1)
        dlog = b[:, :, None, :] - b[:, None, :, :] + lic[:, None, :, :]
        dlog = jnp.where(causal, dlog, -jnp.inf)
        inter = b + m[:, None, :]
        m_t = jnp.maximum(inter, jnp.max(dlog, axis=2))
        s_qk = jnp.einsum('bthd,bshd->btsh', qc, kc) * jnp.exp(dlog - m_t[:, :, None, :])
        sc = jnp.exp(inter - m_t)
        num = jnp.einsum('btsh,bshv->bthv', s_qk, vc) + sc[..., None] * jnp.einsum('bthd,bhdv->bthv', qc, C)
        den = jnp.sum(s_qk, axis=2) + sc * jnp.einsum('bthd,bhd->bth', qc, n)
        hc = num / jnp.maximum(jnp.abs(den), jnp.exp(-m_t))[..., None]
        m_new = m_t[:, -1]
        dec = jnp.exp(inter[:, -1] - m_new)
        w_s = jnp.exp(b[:, -1:, :] - b + lic - m_new[:, None, :])
        C_new = dec[..., None, None] * C + jnp.einsum('bsh,bshd,bshv->bhdv', w_s, kc, vc)
        n_new = dec[..., None] * n + jnp.einsum('bsh,bshd->bhd', w_s, kc)
        return (C_new, n_new, m_new), hc

    init = (C0.astype(f32), n0.astype(f32), m0.astype(f32))
    (C, n, m), hs = lax.scan(step, init, tuple(chunk(a) for a in (q, k, v, log_i, log_f)))
    hs = jnp.moveaxis(hs, 0, 1).reshape(Bn, T, H, DV)
    hs = rmsnorm(hs, g_norm) * jax.nn.sigmoid(o.reshape(Bn, T, H, DV).astype(f32))
    y = hs.reshape(Bn, T, H * DV).astype(h.dtype) @ w_out
    return y, (C.astype(h.dtype), n.astype(h.dtype), m.astype(h.dtype))


def sb_attend(q, qpos, segs, bias):
    bh = bias.astype(jnp.float32)[None, :, None, None]
    zs = [jnp.einsum('bqhd,bshd->bhqs', q, k) + bh for k, _, _ in segs]
    masks = [(kp[None, :] < qpos[:, None])[None, None] for _, _, kp in segs]
    l1 = [jnp.where(mk, jax.nn.log_sigmoid(-z), 0.0) for z, mk in zip(zs, masks)]
    out = 0.0
    later = 0.0
    for i in reversed(range(len(segs))):
        acc = lax.cumsum(l1[i], axis=3, reverse=True) - l1[i] + later
        a = jnp.where(masks[i], jnp.exp(jax.nn.log_sigmoid(zs[i]) + acc), 0.0)
        out = out + jnp.einsum('bhqs,bshd->bqhd', a, segs[i][1].astype(jnp.float32))
        later = later + jnp.sum(l1[i], axis=3, keepdims=True)
    return out


def sb_mixer(h, kv_past, w_qkv, w_out, bias):
    Bn, T, _ = h.shape
    qkv = (h @ w_qkv).reshape(Bn, T, 3, B_HEADS, B_DH)
    q = qkv[:, :, 0].astype(jnp.float32) * B_DH ** -0.5
    kv_new = qkv[:, :, 1:]
    P = 0 if kv_past is None else kv_past.shape[1]
    segs = [] if kv_past is None else [(kv_past[:, :, 0], kv_past[:, :, 1], jnp.arange(P))]
    segs.append((kv_new[:, :, 0], kv_new[:, :, 1], P + jnp.arange(T)))
    QB = math.gcd(T, Q_BLOCK)
    nb = T // QB
    qb = jnp.moveaxis(q.reshape(Bn, nb, QB, B_HEADS, B_DH), 1, 0)
    qpos = (P + jnp.arange(T)).reshape(nb, QB)
    o = lax.map(lambda a: sb_attend(a[0], a[1], segs, bias), (qb, qpos))
    o = jnp.moveaxis(o, 0, 1).reshape(Bn, T, B_HEADS * B_DH).astype(h.dtype)
    return o @ w_out, kv_new


def t5_bucket(dist):
    max_exact = N_BUCKETS // 2
    large = max_exact + (jnp.log(jnp.maximum(dist, 1).astype(jnp.float32) / max_exact)
                         / math.log(MAX_DISTANCE / max_exact) * (N_BUCKETS - max_exact)).astype(jnp.int32)
    return jnp.where(dist < max_exact, dist, jnp.minimum(large, N_BUCKETS - 1))


def dilated_mixer(h, bufs, w_qkv, w_out, rel_bias):
    f32 = jnp.float32
    Bn, T, _ = h.shape
    qkv = (h @ w_qkv).reshape(Bn, T, 3, C_NG, C_HPG, C_DH)
    q = qkv[:, :, 0].astype(f32) * C_DH ** -0.5
    srcs, offs, biases, new_bufs = [], [], [], []
    for g, (win, dil) in enumerate(C_GROUPS):
        kv_g = qkv[:, :, 1:, g]
        keep = min(win, T)
        if bufs is not None:
            keep = bufs[g].shape[1]
            kv_g = jnp.concatenate([bufs[g].astype(kv_g.dtype), kv_g], axis=1)
        srcs.append(kv_g)
        offs.append(kv_g.shape[1] - T)
        new_bufs.append(kv_g[:, kv_g.shape[1] - keep:])
        J = win // dil + 1
        bias = rel_bias[t5_bucket(dil * jnp.arange(J))][:, g * C_HPG:(g + 1) * C_HPG]
        biases.append(bias.astype(f32).T)
    QB = math.gcd(T, Q_BLOCK)
    nb = T // QB

    def block(args):
        qb, t0 = args
        outs, lses = [], []
        for g, (win, dil) in enumerate(C_GROUPS):
            J = win // dil + 1
            idx = offs[g] + t0 + jnp.arange(QB)[:, None] - dil * jnp.arange(J)[None, :]
            valid = (idx >= 0)[None, :, None, :]
            kv_sel = srcs[g][:, jnp.maximum(idx, 0)].astype(f32)
            s = jnp.einsum('bqhd,bqjhd->bqhj', qb[:, :, g], kv_sel[:, :, :, 0]) + biases[g]
            s = jnp.where(valid, s, -jnp.inf)
            mx = jnp.max(s, axis=-1, keepdims=True)
            p = jnp.exp(s - mx)
            l = jnp.sum(p, axis=-1, keepdims=True)
            outs.append(jnp.einsum('bqhj,bqjhd->bqhd', p, kv_sel[:, :, :, 1]) / l)
            lses.append(mx[..., 0] + jnp.log(l[..., 0]))
        wts = jax.nn.softmax(jnp.stack(lses, 0), axis=0)
        return jnp.einsum('gbqh,gbqhd->bqhd', wts, jnp.stack(outs, 0))

    qb = jnp.moveaxis(q.reshape(Bn, nb, QB, C_NG, C_HPG, C_DH), 1, 0)
    o = lax.map(block, (qb, jnp.arange(nb) * QB))
    o = jnp.moveaxis(o, 0, 1).reshape(Bn, T, C_HPG * C_DH).astype(h.dtype)
    return o @ w_out, tuple(new_bufs)


def gdn_mixer(h, conv_buf, S0, w_in, conv_w, A_log, dt_bias, g_norm, w_out):
    f32 = jnp.float32
    Bn, T, _ = h.shape
    n_qk = D_QK_HEADS * D_DK
    n_v = D_V_HEADS * D_DV
    proj = h @ w_in
    qkv = proj[..., :D_CONV_CH]
    z = proj[..., D_CONV_CH:D_CONV_CH + n_v]
    b_raw = proj[..., D_CONV_CH + n_v:D_CONV_CH + n_v + D_V_HEADS]
    a_raw = proj[..., D_CONV_CH + n_v + D_V_HEADS:]
    xpad = jnp.concatenate([conv_buf.astype(qkv.dtype), qkv], axis=1)
    conv = jax.nn.silu(sum(xpad[:, i:i + T] * conv_w[i] for i in range(D_CONV)))
    new_buf = xpad[:, T:]
    rep = D_V_HEADS // D_QK_HEADS
    q = l2norm(conv[..., :n_qk].reshape(Bn, T, D_QK_HEADS, D_DK).astype(f32))
    k = l2norm(conv[..., n_qk:2 * n_qk].reshape(Bn, T, D_QK_HEADS, D_DK).astype(f32))
    v = conv[..., 2 * n_qk:].reshape(Bn, T, D_V_HEADS, D_DV).astype(f32)
    q = jnp.repeat(q, rep, axis=2) * D_DK ** -0.5
    k = jnp.repeat(k, rep, axis=2)
    beta = jax.nn.sigmoid(b_raw.astype(f32))
    g = -jnp.exp(A_log.astype(f32)) * jax.nn.softplus(a_raw.astype(f32) + dt_bias.astype(f32))
    L = math.gcd(T, D_CHUNK)
    nC = T // L
    to_c = lambda a: jnp.moveaxis(a.reshape(Bn, nC, L, *a.shape[2:]), 3, 1)
    qc, kc, vc, bc = to_c(q), to_c(k), to_c(v), to_c(beta)
    G = jnp.cumsum(to_c(g), axis=-1)
    tri = jnp.tril(jnp.ones((L, L), dtype=bool))
    tri_s = jnp.tril(jnp.ones((L, L), dtype=bool), -1)
    decay = jnp.exp(jnp.where(tri, G[..., :, None] - G[..., None, :], -jnp.inf))
    kb = kc * bc[..., None]
    lm = jnp.where(tri_s, jnp.einsum('bhcid,bhcjd->bhcij', kb, kc) * decay, 0.0)
    a_mat = lm + jnp.eye(L, dtype=f32)
    U = lax.linalg.triangular_solve(a_mat, vc * bc[..., None], left_side=True, lower=True, unit_diagonal=True)
    W = lax.linalg.triangular_solve(a_mat, kb * jnp.exp(G)[..., None], left_side=True, lower=True, unit_diagonal=True)
    A_in = jnp.einsum('bhcid,bhcjd->bhcij', qc, kc) * decay
    qg = qc * jnp.exp(G)[..., None]
    GL = G[..., -1]
    kd = kc * jnp.exp(GL[..., None] - G)[..., None]

    def step(S, xs):
        qg_c, kd_c, U_c, W_c, A_c, GL_c = xs
        v_new = U_c - jnp.einsum('bhlk,bhkv->bhlv', W_c, S)
        o = jnp.einsum('bhlk,bhkv->bhlv', qg_c, S) + jnp.einsum('bhij,bhjv->bhiv', A_c, v_new)
        S = jnp.exp(GL_c)[..., None, None] * S + jnp.einsum('bhlk,bhlv->bhkv', kd_c, v_new)
        return S, o

    xs = tuple(jnp.moveaxis(a, 2, 0) for a in (qg, kd, U, W, A_in, GL))
    S_fin, o = lax.scan(step, S0.astype(f32), xs)
    o = jnp.moveaxis(o, 0, 2).reshape(Bn, D_V_HEADS, T, D_DV).transpose(0, 2, 1, 3)
    o = rmsnorm(o, g_norm) * jax.nn.silu(z.reshape(Bn, T, D_V_HEADS, D_DV).astype(f32))
    y = o.reshape(Bn, T, n_v).astype(h.dtype) @ w_out
    return y, (new_buf.astype(h.dtype), S_fin.astype(h.dtype))


def swiglu(h, w_gu, w_down):
    gt, up = jnp.split(h @ w_gu, 2, axis=-1)
    return (jax.nn.silu(gt) * up) @ w_down


def moe_swiglu(h, w_router, w_gu, w_down):
    logits = (h @ w_router).astype(jnp.float32)
    top_v, top_i = lax.top_k(logits, TOP_K)
    gates = jax.nn.softmax(top_v, axis=-1)
    comb = jnp.sum(jax.nn.one_hot(top_i, N_EXPERTS, dtype=jnp.float32) * gates[..., None], axis=-2)
    y = jnp.zeros_like(h)
    for e in range(N_EXPERTS):
        y = y + comb[..., e:e + 1].astype(h.dtype) * swiglu(h, w_gu[e], w_down[e])
    return y


def trunk(x, c, past, w):
    Bn = x.shape[0]
    new = {}
    for layer in range(DEPTH):
        mod = (jax.nn.silu(c) @ w['w_ada'][layer] + w['b_ada'][layer]).reshape(Bn, 6, 1, D_MODEL)
        h = rmsnorm(x, w['norm_g'][layer, 0]) * (1 + mod[:, 1]) + mod[:, 0]
        kind = layer % N_MIXERS
        if kind == 0:
            y, new['mlstm'] = mlstm_mixer(h, *past['mlstm'], w['mlstm_w_in'], w['mlstm_b_if'],
                                          w['mlstm_norm_g'], w['mlstm_w_out'])
        elif kind == 1:
            y, new['sb'] = sb_mixer(h, past['sb'], w['sb_w_qkv'], w['sb_w_out'], w['sb_bias'])
        elif kind == 2:
            y, new['dw'] = dilated_mixer(h, past['dw'], w['dw_w_qkv'], w['dw_w_out'], w['rel_bias'])
        else:
            y, new['gdn'] = gdn_mixer(h, *past['gdn'], w['gdn_w_in'], w['gdn_conv_w'], w['gdn_A_log'],
                                      w['gdn_dt_bias'], w['gdn_norm_g'], w['gdn_w_out'])
        x = x + mod[:, 2] * y
        h = rmsnorm(x, w['norm_g'][layer, 1]) * (1 + mod[:, 4]) + mod[:, 3]
        if layer % 2 == 0:
            y = swiglu(h, w['ffn_w_gu'][layer // 2], w['ffn_w_down'][layer // 2])
        else:
            y = moe_swiglu(h, w['moe_router'][layer // 2], w['moe_w_gu'][layer // 2], w['moe_w_down'][layer // 2])
        x = x + mod[:, 5] * y
    return rmsnorm(x, w['final_g']), new


def setup_inputs(seed: int = 0) -> dict:
    key = jax.random.key(seed)
    ks = iter(jax.random.split(key, 64))
    nrm = lambda shape, scale: jax.random.normal(next(ks), shape, jnp.float32) * scale
    uni = lambda shape, lo, hi: jax.random.uniform(next(ks), shape, jnp.float32, lo, hi)
    n_pages = PAST_LEN // PAGE_SIZE
    n_used = DEC_BATCH * n_pages
    n_pool = n_used + max(1, n_used // 4)
    page_table = jax.random.permutation(next(ks), n_pool)[:n_used].reshape(DEC_BATCH, n_pages).astype(jnp.int32)
    dw_len = [min(win, PAST_LEN) for win, _ in C_GROUPS]
    dt = jnp.exp(uni((D_V_HEADS,), math.log(1e-3), math.log(1e-1)))
    return {
        'x_prompt': nrm((BATCH, SEQ, D_MODEL), 1.0),
        'x_sample': nrm((DEC_BATCH, DEC_SEQ, D_MODEL), 1.0),
        'c_prompt': nrm((BATCH, D_MODEL), 1.0),
        'c_sample': nrm((DEC_BATCH, D_MODEL), 1.0),
        'state_mlstm_C': nrm((DEC_BATCH, A_HEADS, A_DK, A_DV), 1.0),
        'state_mlstm_n': nrm((DEC_BATCH, A_HEADS, A_DK), 1.0),
        'state_mlstm_m': nrm((DEC_BATCH, A_HEADS), 1.0),
        'cache_kv_sb': nrm((n_pool, PAGE_SIZE, 2, B_HEADS, B_DH), 1.0),
        'page_table': page_table,
        'cache_kv_dw1': nrm((DEC_BATCH, dw_len[0], 2, C_HPG, C_DH), 1.0),
        'cache_kv_dw2': nrm((DEC_BATCH, dw_len[1], 2, C_HPG, C_DH), 1.0),
        'cache_kv_dw3': nrm((DEC_BATCH, dw_len[2], 2, C_HPG, C_DH), 1.0),
        'state_conv_gdn': nrm((DEC_BATCH, D_CONV - 1, D_CONV_CH), 1.0),
        'state_S_gdn': nrm((DEC_BATCH, D_V_HEADS, D_DK, D_DV), 1.0),
        'w_ada': nrm((DEPTH, D_MODEL, 6 * D_MODEL), 0.5 * D_MODEL ** -0.5),
        'b_ada': nrm((DEPTH, 6 * D_MODEL), 0.1),
        'norm_g': 1.0 + nrm((DEPTH, 2, D_MODEL), 0.1),
        'final_g': 1.0 + nrm((D_MODEL,), 0.1),
        'mlstm_w_in': nrm((D_MODEL, A_IN), D_MODEL ** -0.5),
        'mlstm_b_if': jnp.stack([nrm((A_HEADS,), 0.1), 3.0 + nrm((A_HEADS,), 0.5)]),
        'mlstm_norm_g': 1.0 + nrm((A_HEADS, A_DV), 0.1),
        'mlstm_w_out': nrm((A_HEADS * A_DV, D_MODEL), (A_HEADS * A_DV) ** -0.5),
        'sb_w_qkv': nrm((D_MODEL, 3 * B_HEADS * B_DH), D_MODEL ** -0.5),
        'sb_w_out': nrm((B_HEADS * B_DH, D_MODEL), (B_HEADS * B_DH) ** -0.5),
        'sb_bias': uni((B_HEADS,), -9.0, -5.0),
        'dw_w_qkv': nrm((D_MODEL, 3 * C_HEADS * C_DH), D_MODEL ** -0.5),
        'dw_w_out': nrm((C_HPG * C_DH, D_MODEL), (C_HPG * C_DH) ** -0.5),
        'rel_bias': nrm((N_BUCKETS, C_HEADS), 0.5),
        'gdn_w_in': nrm((D_MODEL, D_IN), D_MODEL ** -0.5),
        'gdn_conv_w': nrm((D_CONV, D_CONV_CH), D_CONV ** -0.5),
        'gdn_A_log': jnp.log(uni((D_V_HEADS,), 1.0, 16.0)),
        'gdn_dt_bias': jnp.log(jnp.expm1(dt)),
        'gdn_norm_g': 1.0 + nrm((D_DV,), 0.1),
        'gdn_w_out': nrm((D_V_HEADS * D_DV, D_MODEL), (D_V_HEADS * D_DV) ** -0.5),
        'ffn_w_gu': nrm((N_DENSE, D_MODEL, 2 * D_FF), D_MODEL ** -0.5),
        'ffn_w_down': nrm((N_DENSE, D_FF, D_MODEL), D_FF ** -0.5),
        'moe_router': nrm((N_MOE, D_MODEL, N_EXPERTS), D_MODEL ** -0.5),
        'moe_w_gu': nrm((N_MOE, N_EXPERTS, D_MODEL, 2 * D_FF_EXPERT), D_MODEL ** -0.5),
        'moe_w_down': nrm((N_MOE, N_EXPERTS, D_FF_EXPERT, D_MODEL), D_FF_EXPERT ** -0.5),
    }


def reference(x_prompt, x_sample, c_prompt, c_sample, state_mlstm_C, state_mlstm_n, state_mlstm_m,
              cache_kv_sb, page_table, cache_kv_dw1, cache_kv_dw2, cache_kv_dw3, state_conv_gdn, state_S_gdn,
              w_ada, b_ada, norm_g, final_g, mlstm_w_in, mlstm_b_if, mlstm_norm_g, mlstm_w_out,
              sb_w_qkv, sb_w_out, sb_bias, dw_w_qkv, dw_w_out, rel_bias, gdn_w_in, gdn_conv_w, gdn_A_log,
              gdn_dt_bias, gdn_norm_g, gdn_w_out, ffn_w_gu, ffn_w_down, moe_router, moe_w_gu, moe_w_down):
    w = dict(w_ada=w_ada, b_ada=b_ada, norm_g=norm_g, final_g=final_g, mlstm_w_in=mlstm_w_in,
             mlstm_b_if=mlstm_b_if, mlstm_norm_g=mlstm_norm_g, mlstm_w_out=mlstm_w_out,
             sb_w_qkv=sb_w_qkv, sb_w_out=sb_w_out, sb_bias=sb_bias, dw_w_qkv=dw_w_qkv, dw_w_out=dw_w_out,
             rel_bias=rel_bias, gdn_w_in=gdn_w_in, gdn_conv_w=gdn_conv_w, gdn_A_log=gdn_A_log,
             gdn_dt_bias=gdn_dt_bias, gdn_norm_g=gdn_norm_g, gdn_w_out=gdn_w_out, ffn_w_gu=ffn_w_gu,
             ffn_w_down=ffn_w_down, moe_router=moe_router, moe_w_gu=moe_w_gu, moe_w_down=moe_w_down)
    f32 = jnp.float32
    Bp = x_prompt.shape[0]
    past_p = {
        'mlstm': (jnp.zeros((Bp, A_HEADS, A_DK, A_DV), f32), jnp.zeros((Bp, A_HEADS, A_DK), f32),
                  jnp.zeros((Bp, A_HEADS), f32)),
        'sb': None,
        'dw': None,
        'gdn': (jnp.zeros((Bp, D_CONV - 1, D_CONV_CH), x_prompt.dtype),
                jnp.zeros((Bp, D_V_HEADS, D_DK, D_DV), f32)),
    }
    Bd = x_sample.shape[0]
    n_pages = page_table.shape[1]
    kv_past = cache_kv_sb[page_table].reshape(Bd, n_pages * PAGE_SIZE, *cache_kv_sb.shape[2:])
    past_s = {
        'mlstm': (state_mlstm_C, state_mlstm_n, state_mlstm_m),
        'sb': kv_past,
        'dw': (cache_kv_dw1, cache_kv_dw2, cache_kv_dw3),
        'gdn': (state_conv_gdn, state_S_gdn),
    }
    y_prompt, new_p = trunk(x_prompt, c_prompt, past_p, w)
    y_sample, new_s = trunk(x_sample, c_sample, past_s, w)
    C_p, n_p, m_p = new_p['mlstm']
    C_s, n_s, m_s = new_s['mlstm']
    kv_sb_p = new_p['sb']
    kv_sb_s = new_s['sb']
    dw1_p, dw2_p, dw3_p = new_p['dw']
    dw1_s, dw2_s, dw3_s = new_s['dw']
    conv_p, S_p = new_p['gdn']
    conv_s, S_s = new_s['gdn']
    return (y_prompt, y_sample, C_p, n_p, m_p, C_s, n_s, m_s, kv_sb_p, kv_sb_s,
            dw1_p, dw2_p, dw3_p, dw1_s, dw2_s, dw3_s, conv_p, S_p, conv_s, S_s)
```

```python
import functools
import math

import jax
import jax.numpy as jnp
import numpy as np
from jax import lax
from jax.experimental import pallas as pl
from jax.experimental.pallas import tpu as pltpu

F32 = jnp.float32
BF16 = jnp.bfloat16
HI = lax.Precision.HIGHEST

D_MODEL = 2048
EPS = 1e-6
A_HEADS, A_DK, A_DV, A_CHUNK = 8, 128, 256, 64
B_HEADS, B_DH = 16, 128
Q_BLOCK = 128
C_GROUPS = ((128, 1), (512, 4), (2048, 16))
C_NG, C_HPG, C_DH = 3, 8, 128
N_BUCKETS, MAX_DISTANCE = 32, 2048
D_QK_HEADS, D_V_HEADS, D_DK, D_DV, D_CONV, D_CHUNK = 16, 32, 128, 128, 4, 64
D_CONV_CH = 2 * D_QK_HEADS * D_DK + D_V_HEADS * D_DV
D_FF, N_EXPERTS, D_FF_EXPERT = 5632, 8, 7168
PAGE_SIZE = 128
LANE = 128

VMEM_LIMIT_BYTES = 56 * 1024 * 1024
WEIGHT_BLOCK_BYTES = 8 * 1024 * 1024
NEG_BIG = -1e30


def _cp(*sem):
    return pltpu.CompilerParams(dimension_semantics=sem, vmem_limit_bytes=VMEM_LIMIT_BYTES)


def _dot(a, b):
    return jnp.dot(a.astype(BF16), b.astype(BF16), preferred_element_type=F32)


def _dot_nt(a, b):
    return lax.dot_general(a.astype(BF16), b.astype(BF16), (((1,), (1,)), ((), ())),
                           preferred_element_type=F32)


def _dot_tn(a, b):
    return lax.dot_general(a.astype(BF16), b.astype(BF16), (((0,), (0,)), ((), ())),
                           preferred_element_type=F32)


def _dot_hi(a, b):
    return jnp.dot(a, b, precision=HI, preferred_element_type=F32)


def _sigmoid(x):
    return 1.0 / (1.0 + jnp.exp(-x))


def _log_sigmoid(x):
    return jnp.minimum(x, 0.0) - jnp.log1p(jnp.exp(-jnp.abs(x)))


def _softplus(x):
    return jnp.maximum(x, 0.0) + jnp.log1p(jnp.exp(-jnp.abs(x)))


def _iota2(shape, axis):
    return lax.broadcasted_iota(jnp.int32, shape, axis)


def _row_to_col(row, eye):
    return jnp.sum(jnp.where(eye, row, 0.0), axis=1, keepdims=True)


def _pick_tn(K, N, col0=0):
    for tn in (2048, 1024, 512, 256, 128):
        if N % tn == 0 and col0 % tn == 0 and K * tn * 4 <= WEIGHT_BLOCK_BYTES:
            return tn
    raise ValueError((K, N, col0))


def _pick_tm(M):
    return 512 if M % 512 == 0 else M


def _linear_kernel(x_ref, w_ref, o_ref, wbf_ref):
    @pl.when(pl.program_id(1) == 0)
    def _():
        wbf_ref[...] = w_ref[...].astype(BF16)

    o_ref[...] = jnp.dot(x_ref[...], wbf_ref[...], preferred_element_type=F32).astype(o_ref.dtype)


def _linear(x, w, *, sel=(), col0=0, ncols=None, out_dtype=F32):
    M, K = x.shape
    assert w.shape[-2] == K
    N = w.shape[-1] - col0 if ncols is None else ncols
    tn, tm = _pick_tn(K, N, col0), _pick_tm(M)
    off = col0 // tn
    w_spec = pl.BlockSpec((None,) * len(sel) + (K, tn), lambda j, m: tuple(sel) + (0, j + off))
    return pl.pallas_call(
        _linear_kernel, grid=(N // tn, M // tm),
        in_specs=[pl.BlockSpec((tm, K), lambda j, m: (m, 0)), w_spec],
        out_specs=pl.BlockSpec((tm, tn), lambda j, m: (m, j)),
        out_shape=jax.ShapeDtypeStruct((M, N), out_dtype),
        scratch_shapes=[pltpu.VMEM((K, tn), BF16)],
        compiler_params=_cp("arbitrary", "arbitrary"), name="linear",
    )(x, w)


def _gu_kernel(x_ref, wg_ref, wu_ref, o_ref, wg_bf, wu_bf):
    @pl.when(pl.program_id(1) == 0)
    def _():
        wg_bf[...] = wg_ref[...].astype(BF16)
        wu_bf[...] = wu_ref[...].astype(BF16)

    x = x_ref[...]
    g = jnp.dot(x, wg_bf[...], preferred_element_type=F32)
    u = jnp.dot(x, wu_bf[...], preferred_element_type=F32)
    o_ref[...] = (g * _sigmoid(g) * u).astype(o_ref.dtype)


def _swiglu_up(x, w, sel, F):
    M, K = x.shape
    tn, tm = 512, _pick_tm(M)
    lead = (None,) * len(sel)
    nb = F // tn
    return pl.pallas_call(
        _gu_kernel, grid=(nb, M // tm),
        in_specs=[pl.BlockSpec((tm, K), lambda j, m: (m, 0)),
                  pl.BlockSpec(lead + (K, tn), lambda j, m: tuple(sel) + (0, j)),
                  pl.BlockSpec(lead + (K, tn), lambda j, m: tuple(sel) + (0, j + nb))],
        out_specs=pl.BlockSpec((tm, tn), lambda j, m: (m, j)),
        out_shape=jax.ShapeDtypeStruct((M, F), BF16),
        scratch_shapes=[pltpu.VMEM((K, tn), BF16), pltpu.VMEM((K, tn), BF16)],
        compiler_params=_cp("arbitrary", "arbitrary"), name="swiglu_up",
    )(x, w, w)


def _down_acc_kernel(a_ref, w_ref, comb_ref, prev_ref, o_ref, wbf_ref, *, e):
    @pl.when(pl.program_id(1) == 0)
    def _():
        wbf_ref[...] = w_ref[...].astype(BF16)

    y = jnp.dot(a_ref[...], wbf_ref[...], preferred_element_type=F32)
    o_ref[...] = prev_ref[...] + comb_ref[:, e:e + 1] * y


def _down_acc(a, w, sel, comb, prev, e):
    M, K = a.shape
    N = w.shape[-1]
    tn, tm = _pick_tn(K, N), _pick_tm(M)
    lead = (None,) * len(sel)
    return pl.pallas_call(
        functools.partial(_down_acc_kernel, e=e), grid=(N // tn, M // tm),
        in_specs=[pl.BlockSpec((tm, K), lambda j, m: (m, 0)),
                  pl.BlockSpec(lead + (K, tn), lambda j, m: tuple(sel) + (0, j)),
                  pl.BlockSpec((tm, LANE), lambda j, m: (m, 0)),
                  pl.BlockSpec((tm, tn), lambda j, m: (m, j))],
        out_specs=pl.BlockSpec((tm, tn), lambda j, m: (m, j)),
        out_shape=jax.ShapeDtypeStruct((M, N), F32),
        scratch_shapes=[pltpu.VMEM((K, tn), BF16)],
        input_output_aliases={3: 0},
        compiler_params=_cp("arbitrary", "arbitrary"), name="moe_down",
    )(a, w, comb, prev)


def _router_kernel(x_ref, w_ref, o_ref):
    logits = jnp.dot(x_ref[...], w_ref[...].astype(BF16), preferred_element_type=F32)
    lane = _iota2(logits.shape, 1)
    logits = jnp.where(lane < N_EXPERTS, logits, -jnp.inf)
    m1 = jnp.max(logits, axis=1, keepdims=True)
    i1 = jnp.min(jnp.where(logits == m1, lane, LANE), axis=1, keepdims=True)
    rest = jnp.where(lane == i1, -jnp.inf, logits)
    m2 = jnp.max(rest, axis=1, keepdims=True)
    i2 = jnp.min(jnp.where(rest == m2, lane, LANE), axis=1, keepdims=True)
    e2 = jnp.exp(m2 - m1)
    g1 = 1.0 / (1.0 + e2)
    g2 = e2 / (1.0 + e2)
    o_ref[...] = jnp.where(lane == i1, g1, 0.0) + jnp.where(lane == i2, g2, 0.0)


def _router(x, w_pad):
    M, K = x.shape
    tm = _pick_tm(M)
    return pl.pallas_call(
        _router_kernel, grid=(M // tm,),
        in_specs=[pl.BlockSpec((tm, K), lambda m: (m, 0)), pl.BlockSpec((K, LANE), lambda m: (0, 0))],
        out_specs=pl.BlockSpec((tm, LANE), lambda m: (m, 0)),
        out_shape=jax.ShapeDtypeStruct((M, LANE), F32),
        compiler_params=_cp("arbitrary"), name="router",
    )(x, w_pad)


def _ada_kernel(c_ref, w_ref, b_ref, o_ref):
    c = c_ref[...]
    x = (c * _sigmoid(c)).astype(BF16)
    o_ref[0] = jnp.dot(x, w_ref[...].astype(BF16), preferred_element_type=F32) + b_ref[0]


def _ada(c_pad, w_ada, b_ada):
    R = c_pad.shape[0]
    depth, K, N = w_ada.shape
    tn = 1024
    return pl.pallas_call(
        _ada_kernel, grid=(depth, N // tn),
        in_specs=[pl.BlockSpec((R, K), lambda l, j: (0, 0)),
                  pl.BlockSpec((None, K, tn), lambda l, j: (l, 0, j)),
                  pl.BlockSpec((1, 1, tn), lambda l, j: (l, 0, j))],
        out_specs=pl.BlockSpec((1, R, tn), lambda l, j: (l, 0, j)),
        out_shape=jax.ShapeDtypeStruct((depth, R, N), F32),
        compiler_params=_cp("arbitrary", "arbitrary"), name="ada",
    )(c_pad, w_ada, b_ada.reshape(depth, 1, N))


def _norm_kernel(*refs, has_res, mod_rows, gate_row):
    refs = list(refs)
    x_ref = refs.pop(0)
    x = x_ref[0]
    if has_res:
        y_ref, gmod_ref = refs.pop(0), refs.pop(0)
        x = x + gmod_ref[0, gate_row:gate_row + 1, :] * y_ref[0]
    g_ref = refs.pop(0)
    nmod_ref = refs.pop(0) if mod_rows is not None else None
    if has_res and mod_rows is not None:
        xo_ref = refs.pop(0)
        xo_ref[0] = x
    h_ref = refs.pop(0)
    y = x * lax.rsqrt(jnp.mean(x * x, axis=-1, keepdims=True) + EPS) * g_ref[...]
    if mod_rows is not None:
        shift_row, scale_row = mod_rows
        y = y * (1.0 + nmod_ref[0, scale_row:scale_row + 1, :]) + nmod_ref[0, shift_row:shift_row + 1, :]
    h_ref[0] = y.astype(h_ref.dtype)


def _norm(x, g, *, y=None, gmod=None, gate_row=None, nmod=None, mod_rows=None):
    B, T, D = x.shape
    tt = min(T, 256)
    has_res = y is not None
    xs = pl.BlockSpec((1, tt, D), lambda b, t: (b, t, 0))
    ms = pl.BlockSpec((1, 6, D), lambda b, t: (b, 0, 0))
    args, specs = [x], [xs]
    if has_res:
        args += [y, gmod]
        specs += [xs, ms]
    args.append(g.reshape(1, D))
    specs.append(pl.BlockSpec((1, D), lambda b, t: (0, 0)))
    if mod_rows is not None:
        args.append(nmod)
        specs.append(ms)
    h_dtype = BF16 if mod_rows is not None else F32
    out_shape = [jax.ShapeDtypeStruct((B, T, D), h_dtype)]
    out_specs = [xs]
    if has_res and mod_rows is not None:
        out_shape.insert(0, jax.ShapeDtypeStruct((B, T, D), F32))
        out_specs.insert(0, xs)
    out = pl.pallas_call(
        functools.partial(_norm_kernel, has_res=has_res, mod_rows=mod_rows, gate_row=gate_row),
        grid=(B, T // tt), in_specs=specs, out_specs=out_specs, out_shape=out_shape,
        compiler_params=_cp("arbitrary", "arbitrary"), name="norm",
    )(*args)
    return out if len(out) > 1 else out[0]


def _mlstm_kernel(bif_ref, q_ref, k_ref, v_ref, og_ref, gi_ref, gf_ref, c0_ref, n0_ref, m0_ref, gn_ref,
                  hs_ref, c_out, n_out, m_out, C_s, n_s, m_s, *, L, nC):
    hd, c = pl.program_id(1), pl.program_id(2)

    @pl.when(c == 0)
    def _():
        C_s[...] = c0_ref[0, 0]
        n_s[...] = n0_ref[0]
        m_s[...] = m0_ref[0]

    row, col = _iota2((L, L), 0), _iota2((L, L), 1)
    eye, tril = row == col, col <= row
    li_row = gi_ref[0, pl.ds(c, 1), :] + bif_ref[0, hd]
    lf_row = _log_sigmoid(gf_ref[0, pl.ds(c, 1), :] + bif_ref[1, hd])
    li_col, lf_col = _row_to_col(li_row, eye), _row_to_col(lf_row, eye)
    b_col = jnp.sum(jnp.where(tril, lf_row, 0.0), axis=1, keepdims=True)
    b_row = jnp.sum(jnp.where(row <= col, lf_col, 0.0), axis=0, keepdims=True)
    b_last = jnp.sum(lf_row, axis=1, keepdims=True)
    m_prev = m_s[...]
    dlog = jnp.where(tril, b_col - b_row + li_row, -jnp.inf)
    inter = b_col + m_prev
    m_t = jnp.maximum(inter, jnp.max(dlog, axis=1, keepdims=True))
    q = q_ref[0] * (A_DK ** -0.5)
    k, v = k_ref[0], v_ref[0]
    C, n = C_s[...], n_s[...]
    s_qk = _dot_nt(q, k) * jnp.exp(dlog - m_t)
    sc = jnp.exp(inter - m_t)
    num = _dot(s_qk, v) + sc * _dot(q, C)
    den = jnp.sum(s_qk, axis=1, keepdims=True) + sc * jnp.sum(q * n, axis=1, keepdims=True)
    hc = num / jnp.maximum(jnp.abs(den), jnp.exp(-m_t))
    m_new = jnp.max(jnp.where(row[:, :1] == L - 1, m_t, -jnp.inf), axis=0, keepdims=True)
    dec = jnp.exp(b_last + m_prev - m_new)
    w_col = jnp.exp(b_last - b_col + li_col - m_new)
    kw = k * w_col
    C_new = dec * C + _dot_tn(kw, v)
    n_new = dec * n + jnp.sum(kw, axis=0, keepdims=True)
    C_s[...] = C_new
    n_s[...] = n_new
    m_s[...] = m_new
    hn = hc * lax.rsqrt(jnp.mean(hc * hc, axis=-1, keepdims=True) + EPS) * gn_ref[0]
    hs_ref[0] = (hn * _sigmoid(og_ref[0])).astype(hs_ref.dtype)

    @pl.when(c == nC - 1)
    def _():
        c_out[0, 0] = C_new
        n_out[0] = n_new
        m_out[0] = m_new


def _mlstm(h, C0, n0, m0, w):
    B, T, D = h.shape
    H, DK, DV = A_HEADS, A_DK, A_DV
    h2 = h.reshape(B * T, D)
    n_main = 2 * H * DK + 2 * H * DV
    proj = _linear(h2, w['mlstm_w_in'], ncols=n_main).reshape(B, T, n_main)
    w_gate = jnp.pad(w['mlstm_w_in'][:, n_main:], ((0, 0), (0, LANE - 2 * H)))
    gates = _linear(h2, w_gate)
    L = math.gcd(T, A_CHUNK)
    nC = T // L
    to_rows = lambda a: a.reshape(B, nC, L, H).transpose(0, 3, 1, 2).reshape(B * H, nC, L)
    gi, gf = to_rows(gates[:, :H]), to_rows(gates[:, H:2 * H])
    kq, kv = H * DK // DK, 2 * H * DK // DV
    qs = lambda off, wd: pl.BlockSpec((1, L, wd), lambda b, hd, c: (b, c, off + hd))
    gs = pl.BlockSpec((1, nC, L), lambda b, hd, c: (b * H + hd, 0, 0))
    bh3 = lambda s: pl.BlockSpec((1,) + s, lambda b, hd, c: (b * H + hd, 0, 0))
    cs = pl.BlockSpec((1, 1, DK, DV), lambda b, hd, c: (b, hd, 0, 0))
    hs, C, n, m = pl.pallas_call(
        functools.partial(_mlstm_kernel, L=L, nC=nC), grid=(B, H, nC),
        in_specs=[pl.BlockSpec(memory_space=pltpu.SMEM), qs(0, DK), qs(kq, DK), qs(kv, DV), qs(kv + H, DV),
                  gs, gs, cs, bh3((1, DK)), bh3((1, 1)),
                  pl.BlockSpec((1, 1, DV), lambda b, hd, c: (hd, 0, 0))],
        out_specs=[pl.BlockSpec((1, L, DV), lambda b, hd, c: (b, c, hd)), cs, bh3((1, DK)), bh3((1, 1))],
        out_shape=[jax.ShapeDtypeStruct((B, T, H * DV), BF16), jax.ShapeDtypeStruct((B, H, DK, DV), F32),
                   jax.ShapeDtypeStruct((B * H, 1, DK), F32), jax.ShapeDtypeStruct((B * H, 1, 1), F32)],
        scratch_shapes=[pltpu.VMEM((DK, DV), F32), pltpu.VMEM((1, DK), F32), pltpu.VMEM((1, 1), F32)],
        compiler_params=_cp("arbitrary", "arbitrary", "arbitrary"), name="mlstm",
    )(w['mlstm_b_if'], proj, proj, proj, proj, gi, gf, C0.astype(F32), n0.astype(F32).reshape(B * H, 1, DK),
      m0.astype(F32).reshape(B * H, 1, 1), w['mlstm_norm_g'].reshape(H, 1, DV))
    y = _linear(hs.reshape(B * T, H * DV), w['mlstm_w_out'])
    return y.reshape(B, T, D), (C, n.reshape(B, H, DK), m.reshape(B, H))


def _sb_segment(q, k, v, bias, valid, later, upper):
    z = _dot_nt(q, k) + bias
    ls = _log_sigmoid(z)
    l1 = ls - z
    if valid is not None:
        l1 = jnp.where(valid, l1, 0.0)
    hi = l1.astype(BF16)
    lo = (l1 - hi.astype(F32)).astype(BF16)
    after = jnp.dot(hi, upper, preferred_element_type=F32) + jnp.dot(lo, upper, preferred_element_type=F32)
    a = jnp.exp(ls + after + later)
    if valid is not None:
        a = jnp.where(valid, a, 0.0)
    return _dot(a, v), later + jnp.sum(l1, axis=1, keepdims=True)


def _sb_kernel(bias_ref, q_ref, k_ref, v_ref, o_ref, *, TQ):
    hd, i = pl.program_id(1), pl.program_id(2)
    q = q_ref[0] * (B_DH ** -0.5)
    bias = bias_ref[hd]
    row, col = _iota2((TQ, TQ), 0), _iota2((TQ, TQ), 1)
    upper = jnp.where(row > col, 1.0, 0.0).astype(BF16)

    def body(kk, carry):
        out, later = carry
        start = pl.multiple_of((i - kk) * TQ, TQ)
        k = k_ref[0, pl.ds(start, TQ), :]
        v = v_ref[0, pl.ds(start, TQ), :]
        valid = col < row + kk * TQ
        d, later = _sb_segment(q, k, v, bias, valid, later, upper)
        return out + d, later

    out, _ = lax.fori_loop(0, i + 1, body, (jnp.zeros((TQ, B_DH), F32), jnp.zeros((TQ, 1), F32)))
    o_ref[0] = out.astype(o_ref.dtype)


def _sb_prompt(proj, bias):
    B, T, _ = proj.shape
    H = B_HEADS
    TQ = math.gcd(T, Q_BLOCK)
    kvs = lambda off: pl.BlockSpec((1, T, B_DH), lambda b, hd, i: (b, 0, off + hd))
    return pl.pallas_call(
        functools.partial(_sb_kernel, TQ=TQ), grid=(B, H, T // TQ),
        in_specs=[pl.BlockSpec(memory_space=pltpu.SMEM),
                  pl.BlockSpec((1, TQ, B_DH), lambda b, hd, i: (b, i, hd)), kvs(H), kvs(2 * H)],
        out_specs=pl.BlockSpec((1, TQ, B_DH), lambda b, hd, i: (b, i, hd)),
        out_shape=jax.ShapeDtypeStruct((B, T, H * B_DH), BF16),
        compiler_params=_cp("arbitrary", "arbitrary", "arbitrary"), name="sb_prompt",
    )(bias, proj, proj, proj)


def _sb_dec_kernel(pt_ref, qbd_ref, bias_ref, kvn_ref, page_ref, o_ref, acc_s, later_s, *, n_pages, TN):
    p = pl.program_id(1)
    HD = B_HEADS * B_DH
    R = qbd_ref.shape[1]
    q = qbd_ref[0]
    bias = bias_ref[...]
    row, col = _iota2((PAGE_SIZE, PAGE_SIZE), 0), _iota2((PAGE_SIZE, PAGE_SIZE), 1)
    upper = jnp.where(row > col, 1.0, 0.0).astype(BF16)

    @pl.when(p == 0)
    def _():
        rq, ck = _iota2((R, PAGE_SIZE), 0), _iota2((R, PAGE_SIZE), 1)
        valid = ck < lax.rem(rq, TN)
        kv = kvn_ref[0]
        d, later = _sb_segment(q, kv[:, :HD], kv[:, HD:], bias, valid, jnp.zeros((R, 1), F32), upper)
        acc_s[...] = d
        later_s[...] = later

    kv = page_ref[0]
    d, later = _sb_segment(q, kv[:, :HD], kv[:, HD:], bias, None, later_s[...], upper)
    acc_s[...] += d
    later_s[...] = later

    @pl.when(p == n_pages - 1)
    def _():
        for hd in range(B_HEADS):
            o_ref[0, hd * TN:(hd + 1) * TN, :] = acc_s[hd * TN:(hd + 1) * TN, hd * B_DH:(hd + 1) * B_DH]


def _sb_decode(proj, cache, page_table, bias):
    B, TN, _ = proj.shape
    H, DH = B_HEADS, B_DH
    HD = H * DH
    R = H * TN
    n_pages = page_table.shape[1]
    q = proj[:, :, :HD].reshape(B, TN, H, DH).transpose(0, 2, 1, 3) * (DH ** -0.5)
    qbd = jnp.einsum('bhqd,hg->bhqgd', q, jnp.eye(H, dtype=F32)).reshape(B, R, HD).astype(BF16)
    kvn = jnp.pad(proj[:, :, HD:], ((0, 0), (0, PAGE_SIZE - TN), (0, 0)))
    bias_col = jnp.repeat(bias.astype(F32), TN).reshape(R, 1)
    pages = cache.reshape(cache.shape[0], PAGE_SIZE, 2 * HD)
    grid_spec = pltpu.PrefetchScalarGridSpec(
        num_scalar_prefetch=1, grid=(B, n_pages),
        in_specs=[pl.BlockSpec((1, R, HD), lambda b, p, pt: (b, 0, 0)),
                  pl.BlockSpec((R, 1), lambda b, p, pt: (0, 0)),
                  pl.BlockSpec((1, PAGE_SIZE, 2 * HD), lambda b, p, pt: (b, 0, 0)),
                  pl.BlockSpec((1, PAGE_SIZE, 2 * HD), lambda b, p, pt: (pt[b, n_pages - 1 - p], 0, 0))],
        out_specs=pl.BlockSpec((1, R, DH), lambda b, p, pt: (b, 0, 0)),
        scratch_shapes=[pltpu.VMEM((R, HD), F32), pltpu.VMEM((R, 1), F32)])
    o = pl.pallas_call(
        functools.partial(_sb_dec_kernel, n_pages=n_pages, TN=TN), grid_spec=grid_spec,
        out_shape=jax.ShapeDtypeStruct((B, R, DH), F32),
        compiler_params=_cp("arbitrary", "arbitrary"), name="sb_decode",
    )(page_table, qbd, bias_col, kvn, pages)
    return o.reshape(B, H, TN, DH).transpose(0, 2, 1, 3).reshape(B, TN, HD).astype(BF16)


def _sb_mixer(h, past, w):
    B, T, D = h.shape
    proj = _linear(h.reshape(B * T, D), w['sb_w_qkv']).reshape(B, T, 3 * B_HEADS * B_DH)
    if past is None:
        o = _sb_prompt(proj, w['sb_bias'])
    else:
        o = _sb_decode(proj, past[0], past[1], w['sb_bias'])
    y = _linear(o.reshape(B * T, B_HEADS * B_DH), w['sb_w_out'])
    kv_new = proj[:, :, B_HEADS * B_DH:].reshape(B, T, 2, B_HEADS, B_DH)
    return y.reshape(B, T, D), kv_new


def _t5_bucket(dist):
    max_exact = N_BUCKETS // 2
    large = max_exact + (jnp.log(jnp.maximum(dist, 1).astype(F32) / max_exact)
                         / math.log(MAX_DISTANCE / max_exact) * (N_BUCKETS - max_exact)).astype(jnp.int32)
    return jnp.where(dist < max_exact, dist, jnp.minimum(large, N_BUCKETS - 1))


def _tap_bias(rel_bias, g, taps, valid):
    win, dil = C_GROUPS[g]
    J = win // dil + 1
    valid = jnp.logical_and(valid, jnp.logical_and(taps >= 0, taps < J))
    tab = rel_bias[_t5_bucket(dil * jnp.arange(J))][:, g * C_HPG:(g + 1) * C_HPG].astype(F32)
    vals = jnp.moveaxis(tab[jnp.clip(taps, 0, J - 1)], -1, 0)
    return jnp.where(valid[None], vals, -jnp.inf)


def _band_kernel(q_ref, kp_ref, kc_ref, vp_ref, vc_ref, bias_ref, o_ref, lse_ref, *, TQ):
    i = pl.program_id(3)
    q = q_ref[0] * (C_DH ** -0.5)
    s_c = _dot_nt(q, kc_ref[0]) + bias_ref[0, :, TQ:]
    s_p = _dot_nt(q, kp_ref[0]) + bias_ref[0, :, :TQ]
    s_p = jnp.where(i > 0, s_p, -jnp.inf)
    mx = jnp.maximum(jnp.max(s_c, axis=1, keepdims=True), jnp.max(s_p, axis=1, keepdims=True))
    p_c, p_p = jnp.exp(s_c - mx), jnp.exp(s_p - mx)
    l = jnp.sum(p_c, axis=1, keepdims=True) + jnp.sum(p_p, axis=1, keepdims=True)
    o_ref[0] = (_dot(p_c, vc_ref[0]) + _dot(p_p, vp_ref[0])) / l
    lse_ref[0, 0, 0] = mx + jnp.log(l)


def _dw_prompt_group(proj, rel_bias, g):
    B, T, W3 = proj.shape
    win, dil = C_GROUPS[g]
    H, DH = C_HPG, C_DH
    TQ = win // dil
    Ts = T // dil
    assert Ts % TQ == 0
    nb = W3 // DH
    pv = proj.reshape(B, Ts, dil * W3)
    t_loc, s_loc = jnp.arange(TQ)[:, None], jnp.arange(2 * TQ)[None, :] - TQ
    bias = _tap_bias(rel_bias, g, t_loc - s_loc, jnp.ones((TQ, 2 * TQ), bool))
    blk = lambda which, prev: pl.BlockSpec(
        (1, TQ, DH),
        lambda b, r, hd, i: (b, jnp.maximum(i - 1, 0) if prev else i, r * nb + which * C_NG * H + g * H + hd))
    o, lse = pl.pallas_call(
        functools.partial(_band_kernel, TQ=TQ), grid=(B, dil, H, Ts // TQ),
        in_specs=[blk(0, False), blk(1, True), blk(1, False), blk(2, True), blk(2, False),
                  pl.BlockSpec((1, TQ, 2 * TQ), lambda b, r, hd, i: (hd, 0, 0))],
        out_specs=[pl.BlockSpec((1, TQ, DH), lambda b, r, hd, i: (b, i, r * H + hd)),
                   pl.BlockSpec((1, 1, 1, TQ, 1), lambda b, r, hd, i: (b, r, hd, i, 0))],
        out_shape=[jax.ShapeDtypeStruct((B, Ts, dil * H * DH), F32),
                   jax.ShapeDtypeStruct((B, dil, H, Ts, 1), F32)],
        compiler_params=_cp("arbitrary", "arbitrary", "arbitrary", "arbitrary"), name="dw_band",
    )(pv, pv, pv, pv, pv, bias)
    lse = lse.reshape(B, dil, H, Ts).transpose(0, 3, 1, 2).reshape(B, T, H)
    return o.reshape(B, T, H * DH), lse


def _dw_dec_kernel(qbd_ref, bmn_ref, bm_ref, kvn_ref, buf_ref, o_ref, lse_ref, m_s, l_s, acc_s, *, n_tiles, TN):
    wi = pl.program_id(1)
    HD = C_HPG * C_DH
    q = qbd_ref[0]

    def segment(kv, bm, m_old, l_old, acc_old):
        s = _dot_nt(q, kv[:, :HD]) + bm
        m_new = jnp.maximum(m_old, jnp.max(s, axis=1, keepdims=True))
        alpha = jnp.exp(m_old - m_new)
        p = jnp.exp(s - m_new)
        m_s[...] = m_new
        l_s[...] = alpha * l_old + jnp.sum(p, axis=1, keepdims=True)
        acc_s[...] = alpha * acc_old + _dot(p, kv[:, HD:])

    @pl.when(wi == 0)
    def _():
        R = q.shape[0]
        segment(kvn_ref[0], bmn_ref[...], jnp.full((R, 1), NEG_BIG, F32), jnp.zeros((R, 1), F32),
                jnp.zeros((R, HD), F32))

    segment(buf_ref[0], bm_ref[...], m_s[...], l_s[...], acc_s[...])

    @pl.when(wi == n_tiles - 1)
    def _():
        l = l_s[...]
        lse_ref[0] = m_s[...] + jnp.log(l)
        for hd in range(C_HPG):
            rows = slice(hd * TN, (hd + 1) * TN)
            o_ref[0, rows, :] = acc_s[rows, hd * C_DH:(hd + 1) * C_DH] / l[rows]


def _dw_decode_group(proj, buf, rel_bias, g):
    B, TN, _ = proj.shape
    win, dil = C_GROUPS[g]
    H, DH = C_HPG, C_DH
    HD = H * DH
    R = H * TN
    W = buf.shape[1]
    TW = min(W, 512)
    p6 = proj.reshape(B, TN, 3, C_NG, H, DH)
    q = p6[:, :, 0, g].transpose(0, 2, 1, 3) * (DH ** -0.5)
    qbd = jnp.einsum('bhqd,hg->bhqgd', q, jnp.eye(H, dtype=F32)).reshape(B, R, HD).astype(BF16)
    kvn = jnp.pad(p6[:, :, 1:, g].reshape(B, TN, 2 * HD), ((0, 0), (0, LANE - TN), (0, 0)))
    t = jnp.arange(TN)[:, None]
    dist_buf = W + t - jnp.arange(W)[None, :]
    dist_new = t - jnp.arange(LANE)[None, :]
    bias_of = lambda dist, ok: _tap_bias(rel_bias, g, dist // dil, jnp.logical_and(ok, dist % dil == 0))
    bm = bias_of(dist_buf, jnp.ones_like(dist_buf, bool)).reshape(R, W)
    bmn = bias_of(dist_new, jnp.arange(LANE)[None, :] < TN).reshape(R, LANE)
    o, lse = pl.pallas_call(
        functools.partial(_dw_dec_kernel, n_tiles=W // TW, TN=TN), grid=(B, W // TW),
        in_specs=[pl.BlockSpec((1, R, HD), lambda b, wi: (b, 0, 0)),
                  pl.BlockSpec((R, LANE), lambda b, wi: (0, 0)),
                  pl.BlockSpec((R, TW), lambda b, wi: (0, wi)),
                  pl.BlockSpec((1, LANE, 2 * HD), lambda b, wi: (b, 0, 0)),
                  pl.BlockSpec((1, TW, 2 * HD), lambda b, wi: (b, wi, 0))],
        out_specs=[pl.BlockSpec((1, R, DH), lambda b, wi: (b, 0, 0)),
                   pl.BlockSpec((1, R, 1), lambda b, wi: (b, 0, 0))],
        out_shape=[jax.ShapeDtypeStruct((B, R, DH), F32), jax.ShapeDtypeStruct((B, R, 1), F32)],
        scratch_shapes=[pltpu.VMEM((R, 1), F32), pltpu.VMEM((R, 1), F32), pltpu.VMEM((R, HD), F32)],
        compiler_params=_cp("arbitrary", "arbitrary"), name="dw_decode",
    )(qbd, bmn, bm, kvn, buf.reshape(B, W, 2 * HD))
    o = o.reshape(B, H, TN, DH).transpose(0, 2, 1, 3).reshape(B, TN, HD)
    lse = lse.reshape(B, H, TN).transpose(0, 2, 1)
    return o, lse


def _dw_combine_kernel(o0, o1, o2, l0, l1, l2, out_ref):
    ls = [l0[0], l1[0], l2[0]]
    mx = jnp.maximum(jnp.maximum(ls[0], ls[1]), ls[2])
    es = [jnp.exp(l - mx) for l in ls]
    tot = es[0] + es[1] + es[2]
    ws = [e / tot for e in es]
    for hd in range(C_HPG):
        cols = slice(hd * C_DH, (hd + 1) * C_DH)
        acc = ws[0][:, hd:hd + 1] * o0[0, :, cols]
        acc = acc + ws[1][:, hd:hd + 1] * o1[0, :, cols]
        acc = acc + ws[2][:, hd:hd + 1] * o2[0, :, cols]
        out_ref[0, :, cols] = acc.astype(out_ref.dtype)


def _dw_combine(outs, lses):
    B, T, HD = outs[0].shape
    tt = min(T, 256)
    os_ = pl.BlockSpec((1, tt, HD), lambda b, t: (b, t, 0))
    ls_ = pl.BlockSpec((1, tt, C_HPG), lambda b, t: (b, t, 0))
    return pl.pallas_call(
        _dw_combine_kernel, grid=(B, T // tt), in_specs=[os_] * 3 + [ls_] * 3, out_specs=os_,
        out_shape=jax.ShapeDtypeStruct((B, T, HD), BF16),
        compiler_params=_cp("arbitrary", "arbitrary"), name="dw_combine",
    )(*outs, *lses)


def _dw_mixer(h, bufs, w):
    B, T, D = h.shape
    W3 = 3 * C_NG * C_HPG * C_DH
    proj = _linear(h.reshape(B * T, D), w['dw_w_qkv']).reshape(B, T, W3)
    p6 = proj.reshape(B, T, 3, C_NG, C_HPG, C_DH)
    outs, lses, new_bufs = [], [], []
    for g, (win, dil) in enumerate(C_GROUPS):
        kv_g = p6[:, :, 1:, g]
        if bufs is None:
            o, lse = _dw_prompt_group(proj, w['rel_bias'], g)
            new_bufs.append(kv_g[:, T - min(win, T):])
        else:
            o, lse = _dw_decode_group(proj, bufs[g], w['rel_bias'], g)
            new_bufs.append(jnp.concatenate([bufs[g].astype(F32), kv_g], axis=1)[:, T:])
        outs.append(o)
        lses.append(lse)
    o = _dw_combine(outs, lses)
    y = _linear(o.reshape(B * T, C_HPG * C_DH), w['dw_w_out'])
    return y.reshape(B, T, D), tuple(new_bufs)


def _conv_kernel(x_ref, halo_ref, w_ref, o_ref):
    x = x_ref[0]
    halo = halo_ref[0, 0]
    tt = x.shape[0]
    head = x[:8]
    r8 = _iota2(head.shape, 0)
    acc = x * w_ref[D_CONV - 1:D_CONV, :]
    for s in range(1, D_CONV):
        top = jnp.where(r8 < s, pltpu.roll(halo, s, axis=0), pltpu.roll(head, s, axis=0))
        if tt > 8:
            shifted = jnp.concatenate([top, pltpu.roll(x, s, axis=0)[8:]], axis=0)
        else:
            shifted = top
        acc = acc + shifted * w_ref[D_CONV - 1 - s:D_CONV - s, :]
    o_ref[0] = acc * _sigmoid(acc)


def _gdn_conv(proj, conv_buf, conv_w):
    B, T, _ = proj.shape
    C = D_CONV_CH
    tt, tc = min(T, 256), 1024
    nT = T // tt
    first = jnp.pad(conv_buf.astype(F32), ((0, 0), (8 - (D_CONV - 1), 0), (0, 0)))[:, None]
    if nT > 1:
        tails = proj[:, :, :C].reshape(B, nT, tt, C)[:, :-1, tt - 8:]
        halo = jnp.concatenate([first, tails], axis=1)
    else:
        halo = first
    return pl.pallas_call(
        _conv_kernel, grid=(B, nT, C // tc),
        in_specs=[pl.BlockSpec((1, tt, tc), lambda b, t, c: (b, t, c)),
                  pl.BlockSpec((1, 1, 8, tc), lambda b, t, c: (b, t, 0, c)),
                  pl.BlockSpec((D_CONV, tc), lambda b, t, c: (0, c))],
        out_specs=pl.BlockSpec((1, tt, tc), lambda b, t, c: (b, t, c)),
        out_shape=jax.ShapeDtypeStruct((B, T, C), F32),
        compiler_params=_cp("arbitrary", "arbitrary", "arbitrary"), name="gdn_conv",
    )(proj, halo, conv_w.astype(F32))


def _gdn_kernel(par_ref, q_ref, k_ref, v_ref, z_ref, braw_ref, araw_ref, s0_ref, gn_ref, o_ref, s_out, S_s,
                *, L, nC):
    hd, c = pl.program_id(1), pl.program_id(2)

    @pl.when(c == 0)
    def _():
        S_s[...] = s0_ref[0, 0]

    row, col = _iota2((L, L), 0), _iota2((L, L), 1)
    eye, tril = row == col, col <= row
    q, k, v = q_ref[0], k_ref[0], v_ref[0]
    q = q * lax.rsqrt(jnp.sum(q * q, axis=-1, keepdims=True) + EPS) * (D_DK ** -0.5)
    k = k * lax.rsqrt(jnp.sum(k * k, axis=-1, keepdims=True) + EPS)
    beta_row = _sigmoid(braw_ref[0, pl.ds(c, 1), :])
    g_row = -jnp.exp(par_ref[0, hd]) * _softplus(araw_ref[0, pl.ds(c, 1), :] + par_ref[1, hd])
    beta_col, g_col = _row_to_col(beta_row, eye), _row_to_col(g_row, eye)
    G_col = jnp.sum(jnp.where(tril, g_row, 0.0), axis=1, keepdims=True)
    G_row = jnp.sum(jnp.where(row <= col, g_col, 0.0), axis=0, keepdims=True)
    GL = jnp.sum(g_row, axis=1, keepdims=True)
    decay = jnp.exp(jnp.where(tril, G_col - G_row, -jnp.inf))
    kb = k * beta_col
    lm = jnp.where(col < row, _dot_nt(kb, k) * decay, 0.0)
    N = -lm
    inv = jnp.where(eye, 1.0, 0.0) + N
    for _ in range(int(math.log2(L)) - 1):
        N = _dot_hi(N, N)
        inv = inv + _dot_hi(inv, N)
    eG = jnp.exp(G_col)
    U = _dot_hi(inv, v * beta_col)
    Wm = _dot_hi(inv, kb * eG)
    A_in = _dot_nt(q, k) * decay
    S = S_s[...]
    v_new = U - _dot(Wm, S)
    o = _dot(q * eG, S) + _dot(A_in, v_new)
    S_new = jnp.exp(GL) * S + _dot_tn(k * jnp.exp(GL - G_col), v_new)
    S_s[...] = S_new
    z = z_ref[0]
    on = o * lax.rsqrt(jnp.mean(o * o, axis=-1, keepdims=True) + EPS) * gn_ref[...]
    o_ref[0] = (on * (z * _sigmoid(z))).astype(o_ref.dtype)

    @pl.when(c == nC - 1)
    def _():
        s_out[0, 0] = S_new


def _gdn_mixer(h, conv_buf, S0, w):
    B, T, D = h.shape
    HQ, HV, DK, DV = D_QK_HEADS, D_V_HEADS, D_DK, D_DV
    C = D_CONV_CH
    n_v = HV * DV
    n_main = C + n_v
    h2 = h.reshape(B * T, D)
    proj = _linear(h2, w['gdn_w_in'], ncols=n_main).reshape(B, T, n_main)
    w_ba = jnp.pad(w['gdn_w_in'][:, n_main:], ((0, 0), (0, LANE - 2 * HV)))
    ba = _linear(h2, w_ba)
    conv = _gdn_conv(proj, conv_buf, w['gdn_conv_w'])
    new_buf = jnp.concatenate([conv_buf.astype(F32), proj[:, :, :C]], axis=1)[:, T:] if T < D_CONV - 1 \
        else proj[:, T - (D_CONV - 1):, :C]
    L = math.gcd(T, D_CHUNK)
    nC = T // L
    to_rows = lambda a: a.reshape(B, nC, L, HV).transpose(0, 3, 1, 2).reshape(B * HV, nC, L)
    braw, araw = to_rows(ba[:, :HV]), to_rows(ba[:, HV:2 * HV])
    par = jnp.stack([w['gdn_A_log'], w['gdn_dt_bias']]).astype(F32)
    rep = HV // HQ
    blk = lambda f: pl.BlockSpec((1, L, DK), lambda b, hd, c: (b, c, f(hd)))
    gs = pl.BlockSpec((1, nC, L), lambda b, hd, c: (b * HV + hd, 0, 0))
    ss = pl.BlockSpec((1, 1, DK, DV), lambda b, hd, c: (b, hd, 0, 0))
    o, S = pl.pallas_call(
        functools.partial(_gdn_kernel, L=L, nC=nC), grid=(B, HV, nC),
        in_specs=[pl.BlockSpec(memory_space=pltpu.SMEM),
                  blk(lambda hd: hd // rep), blk(lambda hd: HQ + hd // rep), blk(lambda hd: 2 * HQ + hd),
                  blk(lambda hd: C // DV + hd), gs, gs, ss,
                  pl.BlockSpec((1, DV), lambda b, hd, c: (0, 0))],
        out_specs=[pl.BlockSpec((1, L, DV), lambda b, hd, c: (b, c, hd)), ss],
        out_shape=[jax.ShapeDtypeStruct((B, T, n_v), BF16), jax.ShapeDtypeStruct((B, HV, DK, DV), F32)],
        scratch_shapes=[pltpu.VMEM((DK, DV), F32)],
        compiler_params=_cp("arbitrary", "arbitrary", "arbitrary"), name="gdn",
    )(par, conv, conv, conv, proj, braw, araw, S0.astype(F32), w['gdn_norm_g'].reshape(1, DV).astype(F32))
    y = _linear(o.reshape(B * T, n_v), w['gdn_w_out'])
    return y.reshape(B, T, D), (new_buf, S)


def _ffn(h, w, i):
    B, T, D = h.shape
    act = _swiglu_up(h.reshape(B * T, D), w['ffn_w_gu'], (i,), D_FF)
    return _linear(act, w['ffn_w_down'], sel=(i,)).reshape(B, T, D)


def _moe(h, w, i):
    B, T, D = h.shape
    h2 = h.reshape(B * T, D)
    w_r = jnp.pad(w['moe_router'][i], ((0, 0), (0, LANE - N_EXPERTS)))
    comb = _router(h2, w_r)
    y = jnp.zeros((B * T, D), F32)
    for e in range(N_EXPERTS):
        act = _swiglu_up(h2, w['moe_w_gu'], (i, e), D_FF_EXPERT)
        y = _down_acc(act, w['moe_w_down'], (i, e), comb, y, e)
    return y.reshape(B, T, D)


def _trunk(x, mod, past, w):
    new = {}
    depth = mod.shape[0]
    ng = w['norm_g']
    h = _norm(x, ng[0, 0], nmod=mod[0], mod_rows=(0, 1))
    for layer in range(depth):
        kind = layer % 4
        if kind == 0:
            y, new['mlstm'] = _mlstm(h, *past['mlstm'], w)
        elif kind == 1:
            y, new['sb'] = _sb_mixer(h, past['sb'], w)
        elif kind == 2:
            y, new['dw'] = _dw_mixer(h, past['dw'], w)
        else:
            y, new['gdn'] = _gdn_mixer(h, *past['gdn'], w)
        x, h = _norm(x, ng[layer, 1], y=y, gmod=mod[layer], gate_row=2, nmod=mod[layer], mod_rows=(3, 4))
        y = _ffn(h, w, layer // 2) if layer % 2 == 0 else _moe(h, w, layer // 2)
        if layer + 1 < depth:
            x, h = _norm(x, ng[layer + 1, 0], y=y, gmod=mod[layer], gate_row=5, nmod=mod[layer + 1], mod_rows=(0, 1))
        else:
            out = _norm(x, w['final_g'], y=y, gmod=mod[layer], gate_row=5)
    return out, new


def kernel(x_prompt, x_sample, c_prompt, c_sample, state_mlstm_C, state_mlstm_n, state_mlstm_m, cache_kv_sb, page_table, cache_kv_dw1, cache_kv_dw2, cache_kv_dw3, state_conv_gdn, state_S_gdn, w_ada, b_ada, norm_g, final_g, mlstm_w_in, mlstm_b_if, mlstm_norm_g, mlstm_w_out, sb_w_qkv, sb_w_out, sb_bias, dw_w_qkv, dw_w_out, rel_bias, gdn_w_in, gdn_conv_w, gdn_A_log, gdn_dt_bias, gdn_norm_g, gdn_w_out, ffn_w_gu, ffn_w_down, moe_router, moe_w_gu, moe_w_down):
    w = dict(norm_g=norm_g, final_g=final_g, mlstm_w_in=mlstm_w_in, mlstm_b_if=mlstm_b_if,
             mlstm_norm_g=mlstm_norm_g, mlstm_w_out=mlstm_w_out, sb_w_qkv=sb_w_qkv, sb_w_out=sb_w_out,
             sb_bias=sb_bias, dw_w_qkv=dw_w_qkv, dw_w_out=dw_w_out, rel_bias=rel_bias, gdn_w_in=gdn_w_in,
             gdn_conv_w=gdn_conv_w, gdn_A_log=gdn_A_log, gdn_dt_bias=gdn_dt_bias, gdn_norm_g=gdn_norm_g,
             gdn_w_out=gdn_w_out, ffn_w_gu=ffn_w_gu, ffn_w_down=ffn_w_down, moe_router=moe_router,
             moe_w_gu=moe_w_gu, moe_w_down=moe_w_down)
    Bp, Bd = x_prompt.shape[0], x_sample.shape[0]
    depth = w_ada.shape[0]
    rows = -(-(Bp + Bd) // 8) * 8
    c_all = jnp.pad(jnp.concatenate([c_prompt, c_sample], axis=0), ((0, rows - Bp - Bd), (0, 0)))
    mod = _ada(c_all, w_ada, b_ada).reshape(depth, rows, 6, D_MODEL)
    past_p = {
        'mlstm': (jnp.zeros((Bp, A_HEADS, A_DK, A_DV), F32), jnp.zeros((Bp, A_HEADS, A_DK), F32),
                  jnp.zeros((Bp, A_HEADS), F32)),
        'sb': None,
        'dw': None,
        'gdn': (jnp.zeros((Bp, D_CONV - 1, D_CONV_CH), F32), jnp.zeros((Bp, D_V_HEADS, D_DK, D_DV), F32)),
    }
    past_s = {
        'mlstm': (state_mlstm_C, state_mlstm_n, state_mlstm_m),
        'sb': (cache_kv_sb, page_table),
        'dw': (cache_kv_dw1, cache_kv_dw2, cache_kv_dw3),
        'gdn': (state_conv_gdn, state_S_gdn),
    }
    y_prompt, new_p = _trunk(x_prompt, mod[:, :Bp], past_p, w)
    y_sample, new_s = _trunk(x_sample, mod[:, Bp:Bp + Bd], past_s, w)
    C_p, n_p, m_p = new_p['mlstm']
    C_s, n_s, m_s = new_s['mlstm']
    dw1_p, dw2_p, dw3_p = new_p['dw']
    dw1_s, dw2_s, dw3_s = new_s['dw']
    conv_p, S_p = new_p['gdn']
    conv_s, S_s = new_s['gdn']
    return (y_prompt, y_sample, C_p, n_p, m_p, C_s, n_s, m_s, new_p['sb'], new_s['sb'],
            dw1_p, dw2_p, dw3_p, dw1_s, dw2_s, dw3_s, conv_p, S_p, conv_s, S_s)
```

```python
import functools
import math

import jax
import jax.numpy as jnp
import numpy as np
from jax import lax
from jax.experimental import pallas as pl
from jax.experimental.pallas import tpu as pltpu

F32 = jnp.float32
BF16 = jnp.bfloat16
HI = lax.Precision.HIGHEST

D_MODEL = 2048
EPS = 1e-6
A_HEADS, A_DK, A_DV, A_CHUNK = 8, 128, 256, 64
B_HEADS, B_DH = 16, 128
Q_BLOCK = 128
C_GROUPS = ((128, 1), (512, 4), (2048, 16))
C_NG, C_HPG, C_DH = 3, 8, 128
N_BUCKETS, MAX_DISTANCE = 32, 2048
D_QK_HEADS, D_V_HEADS, D_DK, D_DV, D_CONV, D_CHUNK = 16, 32, 128, 128, 4, 64
D_CONV_CH = 2 * D_QK_HEADS * D_DK + D_V_HEADS * D_DV
D_FF, N_EXPERTS, TOP_K, D_FF_EXPERT = 5632, 8, 2, 7168
MOE_TM = 256
DMA_LAG = 32
PAGE_SIZE = 128
LANE = 128
SB_TILE = 256

VMEM_LIMIT_BYTES = 56 * 1024 * 1024
WEIGHT_BLOCK_BYTES = 8 * 1024 * 1024
NEG_BIG = -1e30


def _cp(*sem):
    return pltpu.CompilerParams(dimension_semantics=sem, vmem_limit_bytes=VMEM_LIMIT_BYTES)


def _dot(a, b):
    return jnp.dot(a.astype(BF16), b.astype(BF16), preferred_element_type=F32)


def _dot_nt(a, b):
    return lax.dot_general(a.astype(BF16), b.astype(BF16), (((1,), (1,)), ((), ())),
                           preferred_element_type=F32)


def _dot_tn(a, b):
    return lax.dot_general(a.astype(BF16), b.astype(BF16), (((0,), (0,)), ((), ())),
                           preferred_element_type=F32)


def _dot_hi(a, b):
    return jnp.dot(a, b, precision=HI, preferred_element_type=F32)


def _sigmoid(x):
    return 1.0 / (1.0 + jnp.exp(-x))


def _log_sigmoid(x):
    return jnp.minimum(x, 0.0) - jnp.log1p(jnp.exp(-jnp.abs(x)))


def _softplus(x):
    return jnp.maximum(x, 0.0) + jnp.log1p(jnp.exp(-jnp.abs(x)))


def _iota2(shape, axis):
    return lax.broadcasted_iota(jnp.int32, shape, axis)


def _row_to_col(row, eye):
    return jnp.sum(jnp.where(eye, row, 0.0), axis=1, keepdims=True)


def _pick_tn(K, N, col0=0):
    for tn in (2048, 1024, 512, 256, 128):
        if N % tn == 0 and col0 % tn == 0 and K * tn * 4 <= WEIGHT_BLOCK_BYTES:
            return tn
    raise ValueError((K, N, col0))


def _pick_tm(M):
    return 512 if M % 512 == 0 else M


def _pick_rows(M, cap=1024):
    best = M
    for t in range(8, min(M, cap) + 1, 8):
        if M % t == 0:
            best = t
    return best


def _linear_kernel(x_ref, w_ref, o_ref, wbf_ref):
    @pl.when(pl.program_id(1) == 0)
    def _():
        wbf_ref[...] = w_ref[...].astype(BF16)

    o_ref[...] = jnp.dot(x_ref[...], wbf_ref[...], preferred_element_type=F32).astype(o_ref.dtype)


def _linear(x, w, *, sel=(), col0=0, ncols=None, out_dtype=F32):
    M, K = x.shape
    assert w.shape[-2] == K
    N = w.shape[-1] - col0 if ncols is None else ncols
    tn, tm = _pick_tn(K, N, col0), _pick_tm(M)
    off = col0 // tn
    w_spec = pl.BlockSpec((None,) * len(sel) + (K, tn), lambda j, m: tuple(sel) + (0, j + off))
    return pl.pallas_call(
        _linear_kernel, grid=(N // tn, M // tm),
        in_specs=[pl.BlockSpec((tm, K), lambda j, m: (m, 0)), w_spec],
        out_specs=pl.BlockSpec((tm, tn), lambda j, m: (m, j)),
        out_shape=jax.ShapeDtypeStruct((M, N), out_dtype),
        scratch_shapes=[pltpu.VMEM((K, tn), BF16)],
        compiler_params=_cp("arbitrary", "arbitrary"), name="linear",
    )(x, w)


def _gu_kernel(x_ref, wg_ref, wu_ref, o_ref, wg_bf, wu_bf):
    @pl.when(pl.program_id(1) == 0)
    def _():
        wg_bf[...] = wg_ref[...].astype(BF16)
        wu_bf[...] = wu_ref[...].astype(BF16)

    x = x_ref[...]
    g = jnp.dot(x, wg_bf[...], preferred_element_type=F32)
    u = jnp.dot(x, wu_bf[...], preferred_element_type=F32)
    o_ref[...] = (g * _sigmoid(g) * u).astype(o_ref.dtype)


def _swiglu_up(x, w, sel, F):
    M, K = x.shape
    tn, tm = 512, _pick_tm(M)
    lead = (None,) * len(sel)
    nb = F // tn
    return pl.pallas_call(
        _gu_kernel, grid=(nb, M // tm),
        in_specs=[pl.BlockSpec((tm, K), lambda j, m: (m, 0)),
                  pl.BlockSpec(lead + (K, tn), lambda j, m: tuple(sel) + (0, j)),
                  pl.BlockSpec(lead + (K, tn), lambda j, m: tuple(sel) + (0, j + nb))],
        out_specs=pl.BlockSpec((tm, tn), lambda j, m: (m, j)),
        out_shape=jax.ShapeDtypeStruct((M, F), BF16),
        scratch_shapes=[pltpu.VMEM((K, tn), BF16), pltpu.VMEM((K, tn), BF16)],
        compiler_params=_cp("arbitrary", "arbitrary"), name="swiglu_up",
    )(x, w, w)


def _router_kernel(x_ref, w_ref, o_ref):
    E = N_EXPERTS
    logits = jnp.dot(x_ref[...].astype(BF16), w_ref[...].astype(BF16), preferred_element_type=F32)
    lane = _iota2(logits.shape, 1)
    logits = jnp.where(lane < E, logits, -jnp.inf)
    m1 = jnp.max(logits, axis=1, keepdims=True)
    i1 = jnp.min(jnp.where(logits == m1, lane, LANE), axis=1, keepdims=True)
    rest = jnp.where(lane == i1, -jnp.inf, logits)
    m2 = jnp.max(rest, axis=1, keepdims=True)
    i2 = jnp.min(jnp.where(rest == m2, lane, LANE), axis=1, keepdims=True)
    e2 = jnp.exp(m2 - m1)
    g1 = 1.0 / (1.0 + e2)
    g2 = e2 / (1.0 + e2)
    o_ref[...] = (jnp.where(lane == i1, 1.0, 0.0) + jnp.where(lane == i2 + E, 1.0, 0.0)
                  + jnp.where(lane == 2 * E, g1, 0.0) + jnp.where(lane == 2 * E + 1, g2, 0.0))


def _router(x, w_pad):
    M, K = x.shape
    tm = _pick_rows(M)
    return pl.pallas_call(
        _router_kernel, grid=(M // tm,),
        in_specs=[pl.BlockSpec((tm, K), lambda m: (m, 0)), pl.BlockSpec((K, LANE), lambda m: (0, 0))],
        out_specs=pl.BlockSpec((tm, LANE), lambda m: (m, 0)),
        out_shape=jax.ShapeDtypeStruct((M, LANE), F32),
        compiler_params=_cp("arbitrary"), name="router",
    )(x, w_pad)


def _ada_kernel(c_ref, w_ref, b_ref, o_ref):
    c = c_ref[...]
    x = (c * _sigmoid(c)).astype(BF16)
    o_ref[0] = jnp.dot(x, w_ref[...].astype(BF16), preferred_element_type=F32) + b_ref[0]


def _ada(c_pad, w_ada, b_ada):
    R = c_pad.shape[0]
    depth, K, N = w_ada.shape
    tn = 1024
    return pl.pallas_call(
        _ada_kernel, grid=(depth, N // tn),
        in_specs=[pl.BlockSpec((R, K), lambda l, j: (0, 0)),
                  pl.BlockSpec((None, K, tn), lambda l, j: (l, 0, j)),
                  pl.BlockSpec((1, 1, tn), lambda l, j: (l, 0, j))],
        out_specs=pl.BlockSpec((1, R, tn), lambda l, j: (l, 0, j)),
        out_shape=jax.ShapeDtypeStruct((depth, R, N), F32),
        compiler_params=_cp("arbitrary", "arbitrary"), name="ada",
    )(c_pad, w_ada, b_ada.reshape(depth, 1, N))


def _norm_kernel(*refs, has_res, mod_rows, gate_row):
    refs = list(refs)
    x_ref = refs.pop(0)
    x = x_ref[0]
    if has_res:
        y_ref, gmod_ref = refs.pop(0), refs.pop(0)
        x = x + gmod_ref[0, gate_row:gate_row + 1, :] * y_ref[0]
    g_ref = refs.pop(0)
    nmod_ref = refs.pop(0) if mod_rows is not None else None
    if has_res and mod_rows is not None:
        xo_ref = refs.pop(0)
        xo_ref[0] = x
    h_ref = refs.pop(0)
    y = x * lax.rsqrt(jnp.mean(x * x, axis=-1, keepdims=True) + EPS) * g_ref[...]
    if mod_rows is not None:
        shift_row, scale_row = mod_rows
        y = y * (1.0 + nmod_ref[0, scale_row:scale_row + 1, :]) + nmod_ref[0, shift_row:shift_row + 1, :]
    h_ref[0] = y.astype(h_ref.dtype)


def _norm(x, g, *, y=None, gmod=None, gate_row=None, nmod=None, mod_rows=None, h_dtype=BF16):
    B, T, D = x.shape
    tt = min(T, 256)
    has_res = y is not None
    xs = pl.BlockSpec((1, tt, D), lambda b, t: (b, t, 0))
    ms = pl.BlockSpec((1, 6, D), lambda b, t: (b, 0, 0))
    args, specs = [x], [xs]
    if has_res:
        args += [y, gmod]
        specs += [xs, ms]
    args.append(g.reshape(1, D))
    specs.append(pl.BlockSpec((1, D), lambda b, t: (0, 0)))
    if mod_rows is not None:
        args.append(nmod)
        specs.append(ms)
    out_shape = [jax.ShapeDtypeStruct((B, T, D), h_dtype)]
    out_specs = [xs]
    if has_res and mod_rows is not None:
        out_shape.insert(0, jax.ShapeDtypeStruct((B, T, D), F32))
        out_specs.insert(0, xs)
    out = pl.pallas_call(
        functools.partial(_norm_kernel, has_res=has_res, mod_rows=mod_rows, gate_row=gate_row),
        grid=(B, T // tt), in_specs=specs, out_specs=out_specs, out_shape=out_shape,
        compiler_params=_cp("arbitrary", "arbitrary"), name="norm",
    )(*args)
    return out if len(out) > 1 else out[0]


def _mlstm_kernel(bif_ref, q_ref, k_ref, v_ref, og_ref, gi_ref, gf_ref, c0_ref, n0_ref, m0_ref, gn_ref,
                  hs_ref, c_out, n_out, m_out, C_s, n_s, m_s, *, L, nC):
    hd, c = pl.program_id(1), pl.program_id(2)

    @pl.when(c == 0)
    def _():
        C_s[...] = c0_ref[0, 0]
        n_s[...] = n0_ref[0]
        m_s[...] = m0_ref[0]

    row, col = _iota2((L, L), 0), _iota2((L, L), 1)
    eye, tril = row == col, col <= row
    li_row = gi_ref[0, pl.ds(c, 1), :] + bif_ref[0, hd]
    lf_row = _log_sigmoid(gf_ref[0, pl.ds(c, 1), :] + bif_ref[1, hd])
    li_col, lf_col = _row_to_col(li_row, eye), _row_to_col(lf_row, eye)
    b_col = jnp.sum(jnp.where(tril, lf_row, 0.0), axis=1, keepdims=True)
    b_row = jnp.sum(jnp.where(row <= col, lf_col, 0.0), axis=0, keepdims=True)
    b_last = jnp.sum(lf_row, axis=1, keepdims=True)
    m_prev = m_s[...]
    dlog = jnp.where(tril, b_col - b_row + li_row, -jnp.inf)
    inter = b_col + m_prev
    m_t = jnp.maximum(inter, jnp.max(dlog, axis=1, keepdims=True))
    q = q_ref[0] * (A_DK ** -0.5)
    k, v = k_ref[0], v_ref[0]
    C, n = C_s[...], n_s[...]
    s_qk = _dot_nt(q, k) * jnp.exp(dlog - m_t)
    sc = jnp.exp(inter - m_t)
    num = _dot(s_qk, v) + sc * _dot(q, C)
    den = jnp.sum(s_qk, axis=1, keepdims=True) + sc * jnp.sum(q * n, axis=1, keepdims=True)
    hc = num / jnp.maximum(jnp.abs(den), jnp.exp(-m_t))
    m_new = jnp.max(jnp.where(row[:, :1] == L - 1, m_t, -jnp.inf), axis=0, keepdims=True)
    dec = jnp.exp(b_last + m_prev - m_new)
    w_col = jnp.exp(b_last - b_col + li_col - m_new)
    kw = k * w_col
    C_new = dec * C + _dot_tn(kw, v)
    n_new = dec * n + jnp.sum(kw, axis=0, keepdims=True)
    C_s[...] = C_new
    n_s[...] = n_new
    m_s[...] = m_new
    hn = hc * lax.rsqrt(jnp.mean(hc * hc, axis=-1, keepdims=True) + EPS) * gn_ref[0]
    hs_ref[0] = (hn * _sigmoid(og_ref[0])).astype(hs_ref.dtype)

    @pl.when(c == nC - 1)
    def _():
        c_out[0, 0] = C_new
        n_out[0] = n_new
        m_out[0] = m_new


def _mlstm(h, C0, n0, m0, w):
    B, T, D = h.shape
    H, DK, DV = A_HEADS, A_DK, A_DV
    h2 = h.reshape(B * T, D)
    n_main = 2 * H * DK + 2 * H * DV
    proj = _linear(h2, w['mlstm_w_in'], ncols=n_main).reshape(B, T, n_main)
    w_gate = jnp.pad(w['mlstm_w_in'][:, n_main:], ((0, 0), (0, LANE - 2 * H)))
    gates = _linear(h2, w_gate)
    L = math.gcd(T, A_CHUNK)
    nC = T // L
    to_rows = lambda a: a.reshape(B, nC, L, H).transpose(0, 3, 1, 2).reshape(B * H, nC, L)
    gi, gf = to_rows(gates[:, :H]), to_rows(gates[:, H:2 * H])
    kq, kv = H * DK // DK, 2 * H * DK // DV
    qs = lambda off, wd: pl.BlockSpec((1, L, wd), lambda b, hd, c: (b, c, off + hd))
    gs = pl.BlockSpec((1, nC, L), lambda b, hd, c: (b * H + hd, 0, 0))
    bh3 = lambda s: pl.BlockSpec((1,) + s, lambda b, hd, c: (b * H + hd, 0, 0))
    cs = pl.BlockSpec((1, 1, DK, DV), lambda b, hd, c: (b, hd, 0, 0))
    hs, C, n, m = pl.pallas_call(
        functools.partial(_mlstm_kernel, L=L, nC=nC), grid=(B, H, nC),
        in_specs=[pl.BlockSpec(memory_space=pltpu.SMEM), qs(0, DK), qs(kq, DK), qs(kv, DV), qs(kv + H, DV),
                  gs, gs, cs, bh3((1, DK)), bh3((1, 1)),
                  pl.BlockSpec((1, 1, DV), lambda b, hd, c: (hd, 0, 0))],
        out_specs=[pl.BlockSpec((1, L, DV), lambda b, hd, c: (b, c, hd)), cs, bh3((1, DK)), bh3((1, 1))],
        out_shape=[jax.ShapeDtypeStruct((B, T, H * DV), BF16), jax.ShapeDtypeStruct((B, H, DK, DV), F32),
                   jax.ShapeDtypeStruct((B * H, 1, DK), F32), jax.ShapeDtypeStruct((B * H, 1, 1), F32)],
        scratch_shapes=[pltpu.VMEM((DK, DV), F32), pltpu.VMEM((1, DK), F32), pltpu.VMEM((1, 1), F32)],
        compiler_params=_cp("arbitrary", "arbitrary", "arbitrary"), name="mlstm",
    )(w['mlstm_b_if'], proj, proj, proj, proj, gi, gf, C0.astype(F32), n0.astype(F32).reshape(B * H, 1, DK),
      m0.astype(F32).reshape(B * H, 1, 1), w['mlstm_norm_g'].reshape(H, 1, DV))
    y = _linear(hs.reshape(B * T, H * DV), w['mlstm_w_out'])
    return y.reshape(B, T, D), (C, n.reshape(B, H, DK), m.reshape(B, H))


def _sb_weights(z, valid, later, upper):
    ls = _log_sigmoid(z)
    l1 = ls - z
    if valid is not None:
        l1 = jnp.where(valid, l1, 0.0)
    hi = l1.astype(BF16)
    lo = (l1 - hi.astype(F32)).astype(BF16)
    after = jnp.dot(hi, upper, preferred_element_type=F32) + jnp.dot(lo, upper, preferred_element_type=F32)
    a = jnp.exp(ls + after + later)
    if valid is not None:
        a = jnp.where(valid, a, 0.0)
    return a, later + jnp.sum(l1, axis=1, keepdims=True)


def _sb_kernel(bias_ref, q_ref, k_ref, v_ref, o_ref, *, TQ):
    hd, i = pl.program_id(1), pl.program_id(2)
    q = q_ref[0] * (B_DH ** -0.5)
    bias = bias_ref[hd]
    row, col = _iota2((TQ, TQ), 0), _iota2((TQ, TQ), 1)
    upper = jnp.where(row > col, 1.0, 0.0).astype(BF16)

    def body(kk, carry):
        out, later = carry
        start = pl.multiple_of((i - kk) * TQ, TQ)
        k = k_ref[0, pl.ds(start, TQ), :]
        v = v_ref[0, pl.ds(start, TQ), :]
        valid = col < row + kk * TQ
        a, later = _sb_weights(_dot_nt(q, k) + bias, valid, later, upper)
        return out + _dot(a, v), later

    out, _ = lax.fori_loop(0, i + 1, body, (jnp.zeros((TQ, B_DH), F32), jnp.zeros((TQ, 1), F32)))
    o_ref[0] = out.astype(o_ref.dtype)


def _sb_prompt(proj, bias):
    B, T, _ = proj.shape
    H = B_HEADS
    TQ = math.gcd(T, SB_TILE)
    kvs = lambda off: pl.BlockSpec((1, T, B_DH), lambda b, hd, i: (b, 0, off + hd))
    return pl.pallas_call(
        functools.partial(_sb_kernel, TQ=TQ), grid=(B, H, T // TQ),
        in_specs=[pl.BlockSpec(memory_space=pltpu.SMEM),
                  pl.BlockSpec((1, TQ, B_DH), lambda b, hd, i: (b, i, hd)), kvs(H), kvs(2 * H)],
        out_specs=pl.BlockSpec((1, TQ, B_DH), lambda b, hd, i: (b, i, hd)),
        out_shape=jax.ShapeDtypeStruct((B, T, H * B_DH), BF16),
        compiler_params=_cp("arbitrary", "arbitrary", "arbitrary"), name="sb_prompt",
    )(bias, proj, proj, proj)


def _sb_dec_kernel(pt_ref, q_ref, bias_ref, kvn_ref, page_ref, o_ref, acc_s, later_s, *, n_pages, TN):
    p = pl.program_id(1)
    H = B_HEADS
    R = H * TN
    bias = bias_ref[...]
    row, col = _iota2((PAGE_SIZE, PAGE_SIZE), 0), _iota2((PAGE_SIZE, PAGE_SIZE), 1)
    upper = jnp.where(row > col, 1.0, 0.0).astype(BF16)

    def segment(src_ref, valid, later):
        head_rows = lambda first: src_ref[pl.ds(0, 1), pl.ds(first, PAGE_SIZE, stride=2 * H), :][0]
        z = jnp.concatenate([_dot_nt(q_ref[0, hd], head_rows(hd)) for hd in range(H)], axis=0) + bias
        a, later = _sb_weights(z, valid, later, upper)
        out = jnp.concatenate([_dot(a[hd * TN:(hd + 1) * TN], head_rows(H + hd)) for hd in range(H)], axis=0)
        return out, later

    @pl.when(p == 0)
    def _():
        rq, ck = _iota2((R, PAGE_SIZE), 0), _iota2((R, PAGE_SIZE), 1)
        out, later = segment(kvn_ref, ck < lax.rem(rq, TN), jnp.zeros((R, 1), F32))
        acc_s[...] = out
        later_s[...] = later

    out, later = segment(page_ref, None, later_s[...])
    acc_s[...] += out
    later_s[...] = later

    @pl.when(p == n_pages - 1)
    def _():
        o_ref[0] = acc_s[...]


def _sb_decode(proj, cache, page_table, bias):
    B, TN, _ = proj.shape
    H, DH = B_HEADS, B_DH
    HD = H * DH
    R = H * TN
    n_pages = page_table.shape[1]
    page_rows = PAGE_SIZE * 2 * H
    q = proj[:, :, :HD].reshape(B, TN, H, DH).transpose(0, 2, 1, 3) * (DH ** -0.5)
    kvn = jnp.pad(proj[:, :, HD:].reshape(B, TN * 2 * H, DH), ((0, 0), (0, page_rows - TN * 2 * H), (0, 0)))
    bias_col = jnp.repeat(bias.astype(F32), TN).reshape(R, 1)
    pages = cache.reshape(cache.shape[0], page_rows, DH)
    grid_spec = pltpu.PrefetchScalarGridSpec(
        num_scalar_prefetch=1, grid=(B, n_pages),
        in_specs=[pl.BlockSpec((1, H, TN, DH), lambda b, p, pt: (b, 0, 0, 0)),
                  pl.BlockSpec((R, 1), lambda b, p, pt: (0, 0)),
                  pl.BlockSpec((1, page_rows, DH), lambda b, p, pt: (b, 0, 0)),
                  pl.BlockSpec((1, page_rows, DH), lambda b, p, pt: (pt[b, n_pages - 1 - p], 0, 0))],
        out_specs=pl.BlockSpec((1, R, DH), lambda b, p, pt: (b, 0, 0)),
        scratch_shapes=[pltpu.VMEM((R, DH), F32), pltpu.VMEM((R, 1), F32)])
    o = pl.pallas_call(
        functools.partial(_sb_dec_kernel, n_pages=n_pages, TN=TN), grid_spec=grid_spec,
        out_shape=jax.ShapeDtypeStruct((B, R, DH), F32),
        compiler_params=_cp("arbitrary", "arbitrary"), name="sb_decode",
    )(page_table, q, bias_col, kvn, pages)
    return o.reshape(B, H, TN, DH).transpose(0, 2, 1, 3).reshape(B, TN, HD).astype(BF16)


def _sb_mixer(h, past, w):
    B, T, D = h.shape
    proj = _linear(h.reshape(B * T, D), w['sb_w_qkv']).reshape(B, T, 3 * B_HEADS * B_DH)
    if past is None:
        o = _sb_prompt(proj, w['sb_bias'])
    else:
        o = _sb_decode(proj, past[0], past[1], w['sb_bias'])
    y = _linear(o.reshape(B * T, B_HEADS * B_DH), w['sb_w_out'])
    kv_new = proj[:, :, B_HEADS * B_DH:].reshape(B, T, 2, B_HEADS, B_DH)
    return y.reshape(B, T, D), kv_new


def _t5_bucket(dist):
    max_exact = N_BUCKETS // 2
    large = max_exact + (jnp.log(jnp.maximum(dist, 1).astype(F32) / max_exact)
                         / math.log(MAX_DISTANCE / max_exact) * (N_BUCKETS - max_exact)).astype(jnp.int32)
    return jnp.where(dist < max_exact, dist, jnp.minimum(large, N_BUCKETS - 1))


def _tap_bias(rel_bias, g, taps, valid):
    win, dil = C_GROUPS[g]
    J = win // dil + 1
    valid = jnp.logical_and(valid, jnp.logical_and(taps >= 0, taps < J))
    tab = rel_bias[_t5_bucket(dil * jnp.arange(J))][:, g * C_HPG:(g + 1) * C_HPG].astype(F32)
    vals = jnp.moveaxis(tab[jnp.clip(taps, 0, J - 1)], -1, 0)
    return jnp.where(valid[None], vals, -jnp.inf)


def _band_kernel(q_ref, kp_ref, kc_ref, vp_ref, vc_ref, bias_ref, o_ref, lse_ref, *, TQ):
    i = pl.program_id(3)
    q = q_ref[0] * (C_DH ** -0.5)
    s_c = _dot_nt(q, kc_ref[0]) + bias_ref[0, :, TQ:]
    s_p = _dot_nt(q, kp_ref[0]) + bias_ref[0, :, :TQ]
    s_p = jnp.where(i > 0, s_p, -jnp.inf)
    mx = jnp.maximum(jnp.max(s_c, axis=1, keepdims=True), jnp.max(s_p, axis=1, keepdims=True))
    p_c, p_p = jnp.exp(s_c - mx), jnp.exp(s_p - mx)
    l = jnp.sum(p_c, axis=1, keepdims=True) + jnp.sum(p_p, axis=1, keepdims=True)
    o_ref[0] = (_dot(p_c, vc_ref[0]) + _dot(p_p, vp_ref[0])) / l
    lse_ref[0, 0, 0] = mx + jnp.log(l)


def _dw_prompt_group(proj, rel_bias, g):
    B, T, W3 = proj.shape
    win, dil = C_GROUPS[g]
    H, DH = C_HPG, C_DH
    TQ = win // dil
    Ts = T // dil
    assert Ts % TQ == 0
    nb = W3 // DH
    pv = proj.reshape(B, Ts, dil * W3)
    t_loc, s_loc = jnp.arange(TQ)[:, None], jnp.arange(2 * TQ)[None, :] - TQ
    bias = _tap_bias(rel_bias, g, t_loc - s_loc, jnp.ones((TQ, 2 * TQ), bool))
    blk = lambda which, prev: pl.BlockSpec(
        (1, TQ, DH),
        lambda b, r, hd, i: (b, jnp.maximum(i - 1, 0) if prev else i, r * nb + which * C_NG * H + g * H + hd))
    o, lse = pl.pallas_call(
        functools.partial(_band_kernel, TQ=TQ), grid=(B, dil, H, Ts // TQ),
        in_specs=[blk(0, False), blk(1, True), blk(1, False), blk(2, True), blk(2, False),
                  pl.BlockSpec((1, TQ, 2 * TQ), lambda b, r, hd, i: (hd, 0, 0))],
        out_specs=[pl.BlockSpec((1, TQ, DH), lambda b, r, hd, i: (b, i, r * H + hd)),
                   pl.BlockSpec((1, 1, 1, TQ, 1), lambda b, r, hd, i: (b, r, hd, i, 0))],
        out_shape=[jax.ShapeDtypeStruct((B, Ts, dil * H * DH), F32),
                   jax.ShapeDtypeStruct((B, dil, H, Ts, 1), F32)],
        compiler_params=_cp("arbitrary", "arbitrary", "arbitrary", "arbitrary"), name="dw_band",
    )(pv, pv, pv, pv, pv, bias)
    lse = lse.reshape(B, dil, H, Ts).transpose(0, 3, 1, 2).reshape(B, T, H)
    return o.reshape(B, T, H * DH), lse


def _dw_dec_kernel(q_ref, bmn_ref, bm_ref, kvn_ref, buf_ref, o_ref, lse_ref, m_s, l_s, acc_s, *, n_tiles, TN):
    wi = pl.program_id(1)
    H = C_HPG
    R = H * TN

    def segment(src_ref, bm, m_old, l_old, acc_old):
        n_keys = bm.shape[1]
        head_rows = lambda first: src_ref[pl.ds(0, 1), pl.ds(first, n_keys, stride=2 * H), :][0]
        s = jnp.concatenate([_dot_nt(q_ref[0, hd], head_rows(hd)) for hd in range(H)], axis=0) + bm
        m_new = jnp.maximum(m_old, jnp.max(s, axis=1, keepdims=True))
        alpha = jnp.exp(m_old - m_new)
        p = jnp.exp(s - m_new)
        pv = jnp.concatenate([_dot(p[hd * TN:(hd + 1) * TN], head_rows(H + hd)) for hd in range(H)], axis=0)
        m_s[...] = m_new
        l_s[...] = alpha * l_old + jnp.sum(p, axis=1, keepdims=True)
        acc_s[...] = alpha * acc_old + pv

    @pl.when(wi == 0)
    def _():
        segment(kvn_ref, bmn_ref[...], jnp.full((R, 1), NEG_BIG, F32), jnp.zeros((R, 1), F32),
                jnp.zeros((R, C_DH), F32))

    segment(buf_ref, bm_ref[...], m_s[...], l_s[...], acc_s[...])

    @pl.when(wi == n_tiles - 1)
    def _():
        l = l_s[...]
        lse_ref[0] = m_s[...] + jnp.log(l)
        o_ref[0] = acc_s[...] / l


def _dw_decode_group(proj, buf, rel_bias, g):
    B, TN, _ = proj.shape
    win, dil = C_GROUPS[g]
    H, DH = C_HPG, C_DH
    HD = H * DH
    R = H * TN
    W = buf.shape[1]
    TW = min(W, 512)
    p6 = proj.reshape(B, TN, 3, C_NG, H, DH)
    q = p6[:, :, 0, g].transpose(0, 2, 1, 3) * (DH ** -0.5)
    kvn = jnp.pad(p6[:, :, 1:, g].reshape(B, TN * 2 * H, DH), ((0, 0), (0, (LANE - TN) * 2 * H), (0, 0)))
    t = jnp.arange(TN)[:, None]
    dist_buf = W + t - jnp.arange(W)[None, :]
    dist_new = t - jnp.arange(LANE)[None, :]
    bias_of = lambda dist, ok: _tap_bias(rel_bias, g, dist // dil, jnp.logical_and(ok, dist % dil == 0))
    bm = bias_of(dist_buf, jnp.ones_like(dist_buf, bool)).reshape(R, W)
    bmn = bias_of(dist_new, jnp.arange(LANE)[None, :] < TN).reshape(R, LANE)
    o, lse = pl.pallas_call(
        functools.partial(_dw_dec_kernel, n_tiles=W // TW, TN=TN), grid=(B, W // TW),
        in_specs=[pl.BlockSpec((1, H, TN, DH), lambda b, wi: (b, 0, 0, 0)),
                  pl.BlockSpec((R, LANE), lambda b, wi: (0, 0)),
                  pl.BlockSpec((R, TW), lambda b, wi: (0, wi)),
                  pl.BlockSpec((1, LANE * 2 * H, DH), lambda b, wi: (b, 0, 0)),
                  pl.BlockSpec((1, TW * 2 * H, DH), lambda b, wi: (b, wi, 0))],
        out_specs=[pl.BlockSpec((1, R, DH), lambda b, wi: (b, 0, 0)),
                   pl.BlockSpec((1, R, 1), lambda b, wi: (b, 0, 0))],
        out_shape=[jax.ShapeDtypeStruct((B, R, DH), F32), jax.ShapeDtypeStruct((B, R, 1), F32)],
        scratch_shapes=[pltpu.VMEM((R, 1), F32), pltpu.VMEM((R, 1), F32), pltpu.VMEM((R, DH), F32)],
        compiler_params=_cp("arbitrary", "arbitrary"), name="dw_decode",
    )(q, bmn, bm, kvn, buf.reshape(B, W * 2 * H, DH))
    o = o.reshape(B, H, TN, DH).transpose(0, 2, 1, 3).reshape(B, TN, HD)
    lse = lse.reshape(B, H, TN).transpose(0, 2, 1)
    return o, lse


def _dw_combine_kernel(o0, o1, o2, l0, l1, l2, out_ref):
    ls = [l0[0], l1[0], l2[0]]
    mx = jnp.maximum(jnp.maximum(ls[0], ls[1]), ls[2])
    es = [jnp.exp(l - mx) for l in ls]
    tot = es[0] + es[1] + es[2]
    ws = [e / tot for e in es]
    for hd in range(C_HPG):
        cols = slice(hd * C_DH, (hd + 1) * C_DH)
        acc = ws[0][:, hd:hd + 1] * o0[0, :, cols]
        acc = acc + ws[1][:, hd:hd + 1] * o1[0, :, cols]
        acc = acc + ws[2][:, hd:hd + 1] * o2[0, :, cols]
        out_ref[0, :, cols] = acc.astype(out_ref.dtype)


def _dw_combine(outs, lses):
    B, T, HD = outs[0].shape
    tt = min(T, 256)
    os_ = pl.BlockSpec((1, tt, HD), lambda b, t: (b, t, 0))
    ls_ = pl.BlockSpec((1, tt, C_HPG), lambda b, t: (b, t, 0))
    return pl.pallas_call(
        _dw_combine_kernel, grid=(B, T // tt), in_specs=[os_] * 3 + [ls_] * 3, out_specs=os_,
        out_shape=jax.ShapeDtypeStruct((B, T, HD), BF16),
        compiler_params=_cp("arbitrary", "arbitrary"), name="dw_combine",
    )(*outs, *lses)


def _dw_mixer(h, bufs, w):
    B, T, D = h.shape
    W3 = 3 * C_NG * C_HPG * C_DH
    proj = _linear(h.reshape(B * T, D), w['dw_w_qkv']).reshape(B, T, W3)
    p6 = proj.reshape(B, T, 3, C_NG, C_HPG, C_DH)
    outs, lses, new_bufs = [], [], []
    for g, (win, dil) in enumerate(C_GROUPS):
        kv_g = p6[:, :, 1:, g]
        if bufs is None:
            o, lse = _dw_prompt_group(proj, w['rel_bias'], g)
            new_bufs.append(kv_g[:, T - min(win, T):])
        else:
            o, lse = _dw_decode_group(proj, bufs[g], w['rel_bias'], g)
            new_bufs.append(jnp.concatenate([bufs[g].astype(F32), kv_g], axis=1)[:, T:])
        outs.append(o)
        lses.append(lse)
    o = _dw_combine(outs, lses)
    y = _linear(o.reshape(B * T, C_HPG * C_DH), w['dw_w_out'])
    return y.reshape(B, T, D), tuple(new_bufs)


def _conv_kernel(x_ref, halo_ref, w_ref, o_ref):
    x = x_ref[0]
    halo = halo_ref[0, 0]
    tt = x.shape[0]
    head = x[:8]
    r8 = _iota2(head.shape, 0)
    acc = x * w_ref[D_CONV - 1:D_CONV, :]
    for s in range(1, D_CONV):
        top = jnp.where(r8 < s, pltpu.roll(halo, s, axis=0), pltpu.roll(head, s, axis=0))
        if tt > 8:
            shifted = jnp.concatenate([top, pltpu.roll(x, s, axis=0)[8:]], axis=0)
        else:
            shifted = top
        acc = acc + shifted * w_ref[D_CONV - 1 - s:D_CONV - s, :]
    o_ref[0] = acc * _sigmoid(acc)


def _gdn_conv(proj, conv_buf, conv_w):
    B, T, _ = proj.shape
    C = D_CONV_CH
    tt, tc = min(T, 256), 1024
    nT = T // tt
    first = jnp.pad(conv_buf.astype(F32), ((0, 0), (8 - (D_CONV - 1), 0), (0, 0)))[:, None]
    if nT > 1:
        tails = proj[:, :, :C].reshape(B, nT, tt, C)[:, :-1, tt - 8:]
        halo = jnp.concatenate([first, tails], axis=1)
    else:
        halo = first
    return pl.pallas_call(
        _conv_kernel, grid=(B, nT, C // tc),
        in_specs=[pl.BlockSpec((1, tt, tc), lambda b, t, c: (b, t, c)),
                  pl.BlockSpec((1, 1, 8, tc), lambda b, t, c: (b, t, 0, c)),
                  pl.BlockSpec((D_CONV, tc), lambda b, t, c: (0, c))],
        out_specs=pl.BlockSpec((1, tt, tc), lambda b, t, c: (b, t, c)),
        out_shape=jax.ShapeDtypeStruct((B, T, C), F32),
        compiler_params=_cp("arbitrary", "arbitrary", "arbitrary"), name="gdn_conv",
    )(proj, halo, conv_w.astype(F32))


def _gdn_kernel(par_ref, q_ref, k_ref, v_ref, z_ref, braw_ref, araw_ref, s0_ref, gn_ref, o_ref, s_out, S_s,
                *, L, nC, HG, rep):
    hg, c = pl.program_id(1), pl.program_id(2)
    DK, DV = D_DK, D_DV

    @pl.when(c == 0)
    def _():
        S_s[...] = s0_ref[0].reshape(HG * DK, DV)

    row, col = _iota2((L, L), 0), _iota2((L, L), 1)
    eye, tril = row == col, col <= row
    ident = jnp.where(eye, 1.0, 0.0)
    heads = range(HG)
    qs, ks, kks, qks = [], [], [], []
    for jq in range(HG // rep):
        q, k = q_ref[0, :, jq * DK:(jq + 1) * DK], k_ref[0, :, jq * DK:(jq + 1) * DK]
        q = q * lax.rsqrt(jnp.sum(q * q, axis=-1, keepdims=True) + EPS) * (DK ** -0.5)
        k = k * lax.rsqrt(jnp.sum(k * k, axis=-1, keepdims=True) + EPS)
        qs += [q] * rep
        ks += [k] * rep
        kks += [_dot_nt(k, k)] * rep
        qks += [_dot_nt(q, k)] * rep
    betas, Gs, GLs, decays, Ns = [], [], [], [], []
    for j in heads:
        hd = hg * HG + j
        beta_row = _sigmoid(braw_ref[j, pl.ds(c, 1), :])
        g_row = -jnp.exp(par_ref[0, hd]) * _softplus(araw_ref[j, pl.ds(c, 1), :] + par_ref[1, hd])
        beta_col, g_col = _row_to_col(beta_row, eye), _row_to_col(g_row, eye)
        G_col = jnp.sum(jnp.where(tril, g_row, 0.0), axis=1, keepdims=True)
        G_row = jnp.sum(jnp.where(row <= col, g_col, 0.0), axis=0, keepdims=True)
        decay = jnp.exp(jnp.where(tril, G_col - G_row, -jnp.inf))
        betas.append(beta_col)
        Gs.append(G_col)
        GLs.append(jnp.sum(g_row, axis=1, keepdims=True))
        decays.append(decay)
        Ns.append(jnp.where(col < row, -(beta_col * kks[j]) * decay, 0.0))
    invs = [ident + N for N in Ns]
    for _ in range(int(math.log2(L)) - 1):
        Ns = [_dot_hi(N, N) for N in Ns]
        invs = [inv + _dot_hi(inv, N) for inv, N in zip(invs, Ns)]
    eGs = [jnp.exp(G) for G in Gs]
    Us = [_dot_hi(invs[j], v_ref[0, :, j * DV:(j + 1) * DV] * betas[j]) for j in heads]
    Ws = [_dot_hi(invs[j], ks[j] * (betas[j] * eGs[j])) for j in heads]
    Ss = [S_s[j * DK:(j + 1) * DK, :] for j in heads]
    v_news = [Us[j] - _dot(Ws[j], Ss[j]) for j in heads]
    os_ = [_dot(qs[j] * eGs[j], Ss[j]) + _dot(qks[j] * decays[j], v_news[j]) for j in heads]
    states = [jnp.exp(GLs[j]) * Ss[j] + _dot_tn(ks[j] * jnp.exp(GLs[j] - Gs[j]), v_news[j]) for j in heads]
    outs = []
    for j in heads:
        o, z = os_[j], z_ref[0, :, j * DV:(j + 1) * DV]
        on = o * lax.rsqrt(jnp.mean(o * o, axis=-1, keepdims=True) + EPS) * gn_ref[...]
        outs.append((on * (z * _sigmoid(z))).astype(o_ref.dtype))
    S_new = jnp.concatenate(states, axis=0)
    S_s[...] = S_new
    o_ref[0] = jnp.concatenate(outs, axis=1)

    @pl.when(c == nC - 1)
    def _():
        s_out[0] = S_new.reshape(HG, DK, DV)


def _gdn_mixer(h, conv_buf, S0, w):
    B, T, D = h.shape
    HQ, HV, DK, DV = D_QK_HEADS, D_V_HEADS, D_DK, D_DV
    C = D_CONV_CH
    n_v = HV * DV
    n_main = C + n_v
    h2 = h.reshape(B * T, D)
    proj = _linear(h2, w['gdn_w_in'], ncols=n_main).reshape(B, T, n_main)
    w_ba = jnp.pad(w['gdn_w_in'][:, n_main:], ((0, 0), (0, LANE - 2 * HV)))
    ba = _linear(h2, w_ba)
    conv = _gdn_conv(proj, conv_buf, w['gdn_conv_w'])
    new_buf = jnp.concatenate([conv_buf.astype(F32), proj[:, :, :C]], axis=1)[:, T:] if T < D_CONV - 1 \
        else proj[:, T - (D_CONV - 1):, :C]
    L = math.gcd(T, D_CHUNK)
    nC = T // L
    to_rows = lambda a: a.reshape(B, nC, L, HV).transpose(0, 3, 1, 2).reshape(B * HV, nC, L)
    braw, araw = to_rows(ba[:, :HV]), to_rows(ba[:, HV:2 * HV])
    par = jnp.stack([w['gdn_A_log'], w['gdn_dt_bias']]).astype(F32)
    rep = HV // HQ
    HG = 8
    wq, wv = HG // rep * DK, HG * DV
    blk = lambda wd, off: pl.BlockSpec((1, L, wd), lambda b, hg, c: (b, c, off + hg))
    gs = pl.BlockSpec((HG, nC, L), lambda b, hg, c: (b * (HV // HG) + hg, 0, 0))
    ss = pl.BlockSpec((1, HG, DK, DV), lambda b, hg, c: (b, hg, 0, 0))
    o, S = pl.pallas_call(
        functools.partial(_gdn_kernel, L=L, nC=nC, HG=HG, rep=rep), grid=(B, HV // HG, nC),
        in_specs=[pl.BlockSpec(memory_space=pltpu.SMEM),
                  blk(wq, 0), blk(wq, HQ * DK // wq), blk(wv, 2 * HQ * DK // wv), blk(wv, C // wv), gs, gs, ss,
                  pl.BlockSpec((1, DV), lambda b, hg, c: (0, 0))],
        out_specs=[blk(wv, 0), ss],
        out_shape=[jax.ShapeDtypeStruct((B, T, n_v), BF16), jax.ShapeDtypeStruct((B, HV, DK, DV), F32)],
        scratch_shapes=[pltpu.VMEM((HG * DK, DV), F32)],
        compiler_params=_cp("arbitrary", "arbitrary", "arbitrary"), name="gdn",
    )(par, conv, conv, conv, proj, braw, araw, S0.astype(F32), w['gdn_norm_g'].reshape(1, DV).astype(F32))
    y = _linear(o.reshape(B * T, n_v), w['gdn_w_out'])
    return y.reshape(B, T, D), (new_buf, S)


def _ffn(h, w, i):
    B, T, D = h.shape
    act = _swiglu_up(h.reshape(B * T, D), w['ffn_w_gu'], (i,), D_FF)
    return _linear(act, w['ffn_w_down'], sel=(i,)).reshape(B, T, D)


def _moe_plan(route, n_tiles):
    E, TM = N_EXPERTS, MOE_TM
    sel1, sel2 = route[:, :E], route[:, E:2 * E]
    cnt1 = jnp.sum(sel1, axis=0)
    cnt = cnt1 + jnp.sum(sel2, axis=0)
    pcnt = jnp.ceil(cnt / TM) * TM
    pend = jnp.cumsum(pcnt)
    pstart = pend - pcnt
    rank1 = jnp.cumsum(sel1, axis=0) - sel1
    rank2 = cnt1[None] + jnp.cumsum(sel2, axis=0) - sel2
    dest1 = jnp.sum(sel1 * (pstart[None] + rank1), axis=1)
    dest2 = jnp.sum(sel2 * (pstart[None] + rank2), axis=1)
    dest = jnp.concatenate([dest1, dest2]).astype(jnp.int32)
    n_used = (pend[-1] / TM).astype(jnp.int32)
    first_row = jnp.minimum(jnp.arange(n_tiles), n_used - 1).astype(F32) * TM
    tile_expert = jnp.minimum(jnp.sum((first_row[:, None] >= pend[None, :]).astype(jnp.int32), axis=1), E - 1)
    return dest, tile_expert, n_used.reshape(1)


def _row_copy(src_hbm, src_row, dst, dst_row, sem):
    return pltpu.make_async_copy(src_hbm.at[pl.ds(src_row, 1)], dst.at[pl.ds(dst_row, 1)], sem)


def _lagged_copies(n, copies_of, lag):
    def body(t, carry):
        for cp in copies_of(t):
            cp.start()

        @pl.when(t >= lag)
        def _():
            for cp in copies_of(t - lag):
                cp.wait()
        return carry

    def drain(t, carry):
        for cp in copies_of(t):
            cp.wait()
        return carry

    lax.fori_loop(0, n, body, 0)
    lax.fori_loop(max(n - lag, 0), n, drain, 0)


def _dispatch_kernel(dest_ref, h_hbm, xg_in, xg_out, sem, *, M, TR):
    del xg_in
    base = pl.program_id(0) * TR
    _lagged_copies(TR, lambda r: [_row_copy(h_hbm, base + r, xg_out, dest_ref[k * M + base + r], sem)
                                  for k in range(TOP_K)], DMA_LAG)


def _dispatch(h, dest, n_rows):
    M, D = h.shape
    TR = _pick_rows(M)
    any_spec = pl.BlockSpec(memory_space=pl.ANY)
    grid_spec = pltpu.PrefetchScalarGridSpec(
        num_scalar_prefetch=1, grid=(M // TR,), in_specs=[any_spec, any_spec], out_specs=any_spec,
        scratch_shapes=[pltpu.SemaphoreType.DMA(())])
    return pl.pallas_call(
        functools.partial(_dispatch_kernel, M=M, TR=TR), grid_spec=grid_spec,
        out_shape=jax.ShapeDtypeStruct((n_rows, D), F32), input_output_aliases={2: 0},
        compiler_params=_cp("arbitrary"), name="moe_dispatch",
    )(dest, h, jnp.zeros((n_rows, D), F32))


def _tile_is_new(te_ref, m):
    return jnp.logical_or(m == 0, te_ref[m] != te_ref[jnp.maximum(m - 1, 0)])


def _moe_up_kernel(te_ref, nu_ref, x_ref, wg_ref, wu_ref, o_ref, wg_bf, wu_bf):
    m = pl.program_id(1)

    @pl.when(_tile_is_new(te_ref, m))
    def _():
        wg_bf[...] = wg_ref[...].astype(BF16)
        wu_bf[...] = wu_ref[...].astype(BF16)

    @pl.when(m < nu_ref[0])
    def _():
        x = x_ref[...].astype(BF16)
        g = jnp.dot(x, wg_bf[...], preferred_element_type=F32)
        u = jnp.dot(x, wu_bf[...], preferred_element_type=F32)
        o_ref[...] = (g * _sigmoid(g) * u).astype(o_ref.dtype)

    @pl.when(m >= nu_ref[0])
    def _():
        o_ref[...] = jnp.zeros_like(o_ref)


def _moe_down_kernel(te_ref, nu_ref, a_ref, w_ref, o_ref, w_bf):
    m = pl.program_id(1)

    @pl.when(_tile_is_new(te_ref, m))
    def _():
        w_bf[...] = w_ref[...].astype(BF16)

    @pl.when(m < nu_ref[0])
    def _():
        o_ref[...] = jnp.dot(a_ref[...], w_bf[...], preferred_element_type=F32)

    @pl.when(m >= nu_ref[0])
    def _():
        o_ref[...] = jnp.zeros_like(o_ref)


def _moe_experts(xg, tile_expert, n_used, w, i):
    P, K = xg.shape
    F, TM = D_FF_EXPERT, MOE_TM
    NT = P // TM
    tn = 512
    nb = F // tn
    row = lambda j, m, te, nu: (jnp.minimum(m, nu[0] - 1), 0)
    out = lambda j, m, te, nu: (m, j)
    act = pl.pallas_call(
        _moe_up_kernel,
        grid_spec=pltpu.PrefetchScalarGridSpec(
            num_scalar_prefetch=2, grid=(nb, NT),
            in_specs=[pl.BlockSpec((TM, K), row),
                      pl.BlockSpec((None, None, K, tn), lambda j, m, te, nu: (i, te[m], 0, j)),
                      pl.BlockSpec((None, None, K, tn), lambda j, m, te, nu: (i, te[m], 0, j + nb))],
            out_specs=pl.BlockSpec((TM, tn), out),
            scratch_shapes=[pltpu.VMEM((K, tn), BF16), pltpu.VMEM((K, tn), BF16)]),
        out_shape=jax.ShapeDtypeStruct((P, F), BF16),
        compiler_params=_cp("arbitrary", "arbitrary"), name="moe_up",
    )(tile_expert, n_used, xg, w['moe_w_gu'], w['moe_w_gu'])
    tn = _pick_tn(F, K)
    return pl.pallas_call(
        _moe_down_kernel,
        grid_spec=pltpu.PrefetchScalarGridSpec(
            num_scalar_prefetch=2, grid=(K // tn, NT),
            in_specs=[pl.BlockSpec((TM, F), row),
                      pl.BlockSpec((None, None, F, tn), lambda j, m, te, nu: (i, te[m], 0, j))],
            out_specs=pl.BlockSpec((TM, tn), out),
            scratch_shapes=[pltpu.VMEM((F, tn), BF16)]),
        out_shape=jax.ShapeDtypeStruct((P, K), F32),
        compiler_params=_cp("arbitrary", "arbitrary"), name="moe_down",
    )(tile_expert, n_used, act, w['moe_w_down'])


def _combine_kernel(pos_ref, yg_hbm, route_ref, o_ref, buf, sem, *, M, TR):
    base = pl.program_id(0) * TR
    _lagged_copies(TR, lambda r: [_row_copy(yg_hbm, pos_ref[k * M + base + r], buf.at[k], r, sem.at[k])
                                  for k in range(TOP_K)], DMA_LAG)
    g1 = route_ref[:, 2 * N_EXPERTS:2 * N_EXPERTS + 1]
    g2 = route_ref[:, 2 * N_EXPERTS + 1:2 * N_EXPERTS + 2]
    o_ref[...] = g1 * buf[0] + g2 * buf[1]


def _combine(yg, pos, route):
    M = route.shape[0]
    D = yg.shape[1]
    TR = _pick_rows(M)
    grid_spec = pltpu.PrefetchScalarGridSpec(
        num_scalar_prefetch=1, grid=(M // TR,),
        in_specs=[pl.BlockSpec(memory_space=pl.ANY), pl.BlockSpec((TR, LANE), lambda t, pos: (t, 0))],
        out_specs=pl.BlockSpec((TR, D), lambda t, pos: (t, 0)),
        scratch_shapes=[pltpu.VMEM((TOP_K, TR, D), F32), pltpu.SemaphoreType.DMA((TOP_K,))])
    return pl.pallas_call(
        functools.partial(_combine_kernel, M=M, TR=TR), grid_spec=grid_spec,
        out_shape=jax.ShapeDtypeStruct((M, D), F32),
        compiler_params=_cp("arbitrary"), name="moe_combine",
    )(pos, yg, route)


def _moe(h, w, i):
    M, D = h.shape
    n_tiles = -(-(TOP_K * M + N_EXPERTS * (MOE_TM - 1)) // MOE_TM)
    w_r = jnp.pad(w['moe_router'][i], ((0, 0), (0, LANE - N_EXPERTS)))
    route = _router(h, w_r)
    dest, tile_expert, n_used = _moe_plan(route, n_tiles)
    xg = _dispatch(h, dest, n_tiles * MOE_TM)
    yg = _moe_experts(xg, tile_expert, n_used, w, i)
    return _combine(yg, dest, route)


def _trunk(xs, mods, pasts, w):
    S = range(len(xs))
    news = [{} for _ in S]
    depth = mods[0].shape[0]
    ng = w['norm_g']
    ffn_dtype = lambda layer: BF16 if layer % 2 == 0 else F32
    hs = [_norm(xs[s], ng[0, 0], nmod=mods[s][0], mod_rows=(0, 1)) for s in S]
    outs = [None for _ in S]
    for layer in range(depth):
        kind = layer % 4
        ys = []
        for s in S:
            h, past, new = hs[s], pasts[s], news[s]
            if kind == 0:
                y, new['mlstm'] = _mlstm(h, *past['mlstm'], w)
            elif kind == 1:
                y, new['sb'] = _sb_mixer(h, past['sb'], w)
            elif kind == 2:
                y, new['dw'] = _dw_mixer(h, past['dw'], w)
            else:
                y, new['gdn'] = _gdn_mixer(h, *past['gdn'], w)
            ys.append(y)
        for s in S:
            xs[s], hs[s] = _norm(xs[s], ng[layer, 1], y=ys[s], gmod=mods[s][layer], gate_row=2,
                                 nmod=mods[s][layer], mod_rows=(3, 4), h_dtype=ffn_dtype(layer))
        if layer % 2 == 0:
            ys = [_ffn(hs[s], w, layer // 2) for s in S]
        else:
            sizes = [hs[s].shape[0] * hs[s].shape[1] for s in S]
            y_all = _moe(jnp.concatenate([hs[s].reshape(sizes[s], D_MODEL) for s in S], axis=0), w, layer // 2)
            offs = np.cumsum([0] + sizes)
            ys = [y_all[offs[s]:offs[s + 1]].reshape(hs[s].shape) for s in S]
        for s in S:
            if layer + 1 < depth:
                xs[s], hs[s] = _norm(xs[s], ng[layer + 1, 0], y=ys[s], gmod=mods[s][layer], gate_row=5,
                                     nmod=mods[s][layer + 1], mod_rows=(0, 1))
            else:
                outs[s] = _norm(xs[s], w['final_g'], y=ys[s], gmod=mods[s][layer], gate_row=5, h_dtype=F32)
    return outs, news


def kernel(x_prompt, x_sample, c_prompt, c_sample, state_mlstm_C, state_mlstm_n, state_mlstm_m, cache_kv_sb, page_table, cache_kv_dw1, cache_kv_dw2, cache_kv_dw3, state_conv_gdn, state_S_gdn, w_ada, b_ada, norm_g, final_g, mlstm_w_in, mlstm_b_if, mlstm_norm_g, mlstm_w_out, sb_w_qkv, sb_w_out, sb_bias, dw_w_qkv, dw_w_out, rel_bias, gdn_w_in, gdn_conv_w, gdn_A_log, gdn_dt_bias, gdn_norm_g, gdn_w_out, ffn_w_gu, ffn_w_down, moe_router, moe_w_gu, moe_w_down):
    w = dict(norm_g=norm_g, final_g=final_g, mlstm_w_in=mlstm_w_in, mlstm_b_if=mlstm_b_if,
             mlstm_norm_g=mlstm_norm_g, mlstm_w_out=mlstm_w_out, sb_w_qkv=sb_w_qkv, sb_w_out=sb_w_out,
             sb_bias=sb_bias, dw_w_qkv=dw_w_qkv, dw_w_out=dw_w_out, rel_bias=rel_bias, gdn_w_in=gdn_w_in,
             gdn_conv_w=gdn_conv_w, gdn_A_log=gdn_A_log, gdn_dt_bias=gdn_dt_bias, gdn_norm_g=gdn_norm_g,
             gdn_w_out=gdn_w_out, ffn_w_gu=ffn_w_gu, ffn_w_down=ffn_w_down, moe_router=moe_router,
             moe_w_gu=moe_w_gu, moe_w_down=moe_w_down)
    Bp, Bd = x_prompt.shape[0], x_sample.shape[0]
    depth = w_ada.shape[0]
    rows = -(-(Bp + Bd) // 8) * 8
    c_all = jnp.pad(jnp.concatenate([c_prompt, c_sample], axis=0), ((0, rows - Bp - Bd), (0, 0)))
    mod = _ada(c_all, w_ada, b_ada).reshape(depth, rows, 6, D_MODEL)
    past_p = {
        'mlstm': (jnp.zeros((Bp, A_HEADS, A_DK, A_DV), F32), jnp.zeros((Bp, A_HEADS, A_DK), F32),
                  jnp.zeros((Bp, A_HEADS), F32)),
        'sb': None,
        'dw': None,
        'gdn': (jnp.zeros((Bp, D_CONV - 1, D_CONV_CH), F32), jnp.zeros((Bp, D_V_HEADS, D_DK, D_DV), F32)),
    }
    past_s = {
        'mlstm': (state_mlstm_C, state_mlstm_n, state_mlstm_m),
        'sb': (cache_kv_sb, page_table),
        'dw': (cache_kv_dw1, cache_kv_dw2, cache_kv_dw3),
        'gdn': (state_conv_gdn, state_S_gdn),
    }
    (y_prompt, y_sample), (new_p, new_s) = _trunk(
        [x_prompt, x_sample], [mod[:, :Bp], mod[:, Bp:Bp + Bd]], [past_p, past_s], w)
    C_p, n_p, m_p = new_p['mlstm']
    C_s, n_s, m_s = new_s['mlstm']
    dw1_p, dw2_p, dw3_p = new_p['dw']
    dw1_s, dw2_s, dw3_s = new_s['dw']
    conv_p, S_p = new_p['gdn']
    conv_s, S_s = new_s['gdn']
    return (y_prompt, y_sample, C_p, n_p, m_p, C_s, n_s, m_s, new_p['sb'], new_s['sb'],
            dw1_p, dw2_p, dw3_p, dw1_s, dw2_s, dw3_s, conv_p, S_p, conv_s, S_s)
```

```python
import functools
import math

import jax
import jax.numpy as jnp
import numpy as np
from jax import lax
from jax.experimental import pallas as pl
from jax.experimental.pallas import tpu as pltpu

F32 = jnp.float32
BF16 = jnp.bfloat16
HI = lax.Precision.HIGHEST

D_MODEL = 2048
EPS = 1e-6
A_HEADS, A_DK, A_DV, A_CHUNK = 8, 128, 256, 64
B_HEADS, B_DH = 16, 128
Q_BLOCK = 128
C_GROUPS = ((128, 1), (512, 4), (2048, 16))
C_NG, C_HPG, C_DH = 3, 8, 128
N_BUCKETS, MAX_DISTANCE = 32, 2048
D_QK_HEADS, D_V_HEADS, D_DK, D_DV, D_CONV, D_CHUNK = 16, 32, 128, 128, 4, 64
D_CONV_CH = 2 * D_QK_HEADS * D_DK + D_V_HEADS * D_DV
D_FF, N_EXPERTS, TOP_K, D_FF_EXPERT = 5632, 8, 2, 7168
MOE_TM = 256
DMA_LAG = 32
PAGE_SIZE = 128
LANE = 128
SB_TILE = 256
SB_HEADS_PER_STEP = 2

VMEM_LIMIT_BYTES = 56 * 1024 * 1024
WEIGHT_BLOCK_BYTES = 12 * 1024 * 1024
NEG_BIG = -1e30


def _cp(*sem):
    return pltpu.CompilerParams(dimension_semantics=sem, vmem_limit_bytes=VMEM_LIMIT_BYTES)


def _dot(a, b):
    return jnp.dot(a.astype(BF16), b.astype(BF16), preferred_element_type=F32)


def _dot_nt(a, b):
    return lax.dot_general(a.astype(BF16), b.astype(BF16), (((1,), (1,)), ((), ())),
                           preferred_element_type=F32)


def _dot_tn(a, b):
    return lax.dot_general(a.astype(BF16), b.astype(BF16), (((0,), (0,)), ((), ())),
                           preferred_element_type=F32)


def _dot_hi(a, b):
    return jnp.dot(a, b, precision=HI, preferred_element_type=F32)


def _sigmoid(x):
    return 1.0 / (1.0 + jnp.exp(-x))


def _log_sigmoid(x):
    return jnp.minimum(x, 0.0) - jnp.log1p(jnp.exp(-jnp.abs(x)))


def _softplus(x):
    return jnp.maximum(x, 0.0) + jnp.log1p(jnp.exp(-jnp.abs(x)))


def _iota2(shape, axis):
    return lax.broadcasted_iota(jnp.int32, shape, axis)


def _row_to_col(row, eye):
    return jnp.sum(jnp.where(eye, row, 0.0), axis=1, keepdims=True)


def _pick_tn(K, N, col0=0):
    for tn in (2048, 1024, 512, 256, 128):
        if N % tn == 0 and col0 % tn == 0 and K * tn * 4 <= WEIGHT_BLOCK_BYTES:
            return tn
    raise ValueError((K, N, col0))


def _pick_tm(M):
    return 512 if M % 512 == 0 else M


def _pick_rows(M, cap=1024):
    best = M
    for t in range(8, min(M, cap) + 1, 8):
        if M % t == 0:
            best = t
    return best


def _linear_kernel(x_ref, w_ref, o_ref, wbf_ref):
    @pl.when(pl.program_id(1) == 0)
    def _():
        wbf_ref[...] = w_ref[...].astype(BF16)

    o_ref[...] = jnp.dot(x_ref[...], wbf_ref[...], preferred_element_type=F32).astype(o_ref.dtype)


def _linear(x, w, *, sel=(), col0=0, ncols=None, out_dtype=F32):
    M, K = x.shape
    assert w.shape[-2] == K
    N = w.shape[-1] - col0 if ncols is None else ncols
    tn, tm = _pick_tn(K, N, col0), _pick_tm(M)
    off = col0 // tn
    w_spec = pl.BlockSpec((None,) * len(sel) + (K, tn), lambda j, m: tuple(sel) + (0, j + off))
    return pl.pallas_call(
        _linear_kernel, grid=(N // tn, M // tm),
        in_specs=[pl.BlockSpec((tm, K), lambda j, m: (m, 0)), w_spec],
        out_specs=pl.BlockSpec((tm, tn), lambda j, m: (m, j)),
        out_shape=jax.ShapeDtypeStruct((M, N), out_dtype),
        scratch_shapes=[pltpu.VMEM((K, tn), BF16)],
        compiler_params=_cp("arbitrary", "arbitrary"), name="linear",
    )(x, w)


def _gu_kernel(x_ref, wg_ref, wu_ref, o_ref, wg_bf, wu_bf):
    @pl.when(pl.program_id(1) == 0)
    def _():
        wg_bf[...] = wg_ref[...].astype(BF16)
        wu_bf[...] = wu_ref[...].astype(BF16)

    x = x_ref[...]
    g = jnp.dot(x, wg_bf[...], preferred_element_type=F32)
    u = jnp.dot(x, wu_bf[...], preferred_element_type=F32)
    o_ref[...] = (g * _sigmoid(g) * u).astype(o_ref.dtype)


def _swiglu_up(x, w, sel, F):
    M, K = x.shape
    tn, tm = 512, _pick_tm(M)
    lead = (None,) * len(sel)
    nb = F // tn
    return pl.pallas_call(
        _gu_kernel, grid=(nb, M // tm),
        in_specs=[pl.BlockSpec((tm, K), lambda j, m: (m, 0)),
                  pl.BlockSpec(lead + (K, tn), lambda j, m: tuple(sel) + (0, j)),
                  pl.BlockSpec(lead + (K, tn), lambda j, m: tuple(sel) + (0, j + nb))],
        out_specs=pl.BlockSpec((tm, tn), lambda j, m: (m, j)),
        out_shape=jax.ShapeDtypeStruct((M, F), BF16),
        scratch_shapes=[pltpu.VMEM((K, tn), BF16), pltpu.VMEM((K, tn), BF16)],
        compiler_params=_cp("arbitrary", "arbitrary"), name="swiglu_up",
    )(x, w, w)


def _router_kernel(x_ref, w_ref, o_ref):
    E = N_EXPERTS
    logits = jnp.dot(x_ref[...].astype(BF16), w_ref[...].astype(BF16), preferred_element_type=F32)
    lane = _iota2(logits.shape, 1)
    logits = jnp.where(lane < E, logits, -jnp.inf)
    m1 = jnp.max(logits, axis=1, keepdims=True)
    i1 = jnp.min(jnp.where(logits == m1, lane, LANE), axis=1, keepdims=True)
    rest = jnp.where(lane == i1, -jnp.inf, logits)
    m2 = jnp.max(rest, axis=1, keepdims=True)
    i2 = jnp.min(jnp.where(rest == m2, lane, LANE), axis=1, keepdims=True)
    e2 = jnp.exp(m2 - m1)
    g1 = 1.0 / (1.0 + e2)
    g2 = e2 / (1.0 + e2)
    o_ref[...] = (jnp.where(lane == i1, 1.0, 0.0) + jnp.where(lane == i2 + E, 1.0, 0.0)
                  + jnp.where(lane == 2 * E, g1, 0.0) + jnp.where(lane == 2 * E + 1, g2, 0.0))


def _router(x, w_pad):
    M, K = x.shape
    tm = _pick_rows(M)
    return pl.pallas_call(
        _router_kernel, grid=(M // tm,),
        in_specs=[pl.BlockSpec((tm, K), lambda m: (m, 0)), pl.BlockSpec((K, LANE), lambda m: (0, 0))],
        out_specs=pl.BlockSpec((tm, LANE), lambda m: (m, 0)),
        out_shape=jax.ShapeDtypeStruct((M, LANE), F32),
        compiler_params=_cp("arbitrary"), name="router",
    )(x, w_pad)


def _ada_kernel(c_ref, w_ref, b_ref, o_ref):
    c = c_ref[...]
    x = (c * _sigmoid(c)).astype(BF16)
    o_ref[0] = jnp.dot(x, w_ref[...].astype(BF16), preferred_element_type=F32) + b_ref[0]


def _ada(c_pad, w_ada, b_ada):
    R = c_pad.shape[0]
    depth, K, N = w_ada.shape
    tn = 1024
    return pl.pallas_call(
        _ada_kernel, grid=(depth, N // tn),
        in_specs=[pl.BlockSpec((R, K), lambda l, j: (0, 0)),
                  pl.BlockSpec((None, K, tn), lambda l, j: (l, 0, j)),
                  pl.BlockSpec((1, 1, tn), lambda l, j: (l, 0, j))],
        out_specs=pl.BlockSpec((1, R, tn), lambda l, j: (l, 0, j)),
        out_shape=jax.ShapeDtypeStruct((depth, R, N), F32),
        compiler_params=_cp("arbitrary", "arbitrary"), name="ada",
    )(c_pad, w_ada, b_ada.reshape(depth, 1, N))


def _norm_kernel(*refs, has_res, mod_rows, gate_row):
    refs = list(refs)
    x_ref = refs.pop(0)
    x = x_ref[0]
    if has_res:
        y_ref, gmod_ref = refs.pop(0), refs.pop(0)
        x = x + gmod_ref[0, gate_row:gate_row + 1, :] * y_ref[0]
    g_ref = refs.pop(0)
    nmod_ref = refs.pop(0) if mod_rows is not None else None
    if has_res and mod_rows is not None:
        xo_ref = refs.pop(0)
        xo_ref[0] = x
    h_ref = refs.pop(0)
    y = x * lax.rsqrt(jnp.mean(x * x, axis=-1, keepdims=True) + EPS) * g_ref[...]
    if mod_rows is not None:
        shift_row, scale_row = mod_rows
        y = y * (1.0 + nmod_ref[0, scale_row:scale_row + 1, :]) + nmod_ref[0, shift_row:shift_row + 1, :]
    h_ref[0] = y.astype(h_ref.dtype)


def _norm(x, g, *, y=None, gmod=None, gate_row=None, nmod=None, mod_rows=None, h_dtype=BF16):
    B, T, D = x.shape
    tt = min(T, 256)
    has_res = y is not None
    xs = pl.BlockSpec((1, tt, D), lambda b, t: (b, t, 0))
    ms = pl.BlockSpec((1, 6, D), lambda b, t: (b, 0, 0))
    args, specs = [x], [xs]
    if has_res:
        args += [y, gmod]
        specs += [xs, ms]
    args.append(g.reshape(1, D))
    specs.append(pl.BlockSpec((1, D), lambda b, t: (0, 0)))
    if mod_rows is not None:
        args.append(nmod)
        specs.append(ms)
    out_shape = [jax.ShapeDtypeStruct((B, T, D), h_dtype)]
    out_specs = [xs]
    if has_res and mod_rows is not None:
        out_shape.insert(0, jax.ShapeDtypeStruct((B, T, D), F32))
        out_specs.insert(0, xs)
    out = pl.pallas_call(
        functools.partial(_norm_kernel, has_res=has_res, mod_rows=mod_rows, gate_row=gate_row),
        grid=(B, T // tt), in_specs=specs, out_specs=out_specs, out_shape=out_shape,
        compiler_params=_cp("arbitrary", "arbitrary"), name="norm",
    )(*args)
    return out if len(out) > 1 else out[0]


def _mlstm_kernel(bif_ref, q_ref, k_ref, v_ref, og_ref, gi_ref, gf_ref, c0_ref, n0_ref, m0_ref, gn_ref,
                  hs_ref, c_out, n_out, m_out, C_s, n_s, m_s, *, L, nC):
    c = pl.program_id(1)
    H, DK, DV = A_HEADS, A_DK, A_DV
    heads = range(H)

    @pl.when(c == 0)
    def _():
        C_s[...] = c0_ref[0].reshape(H * DK, DV)
        n_s[...] = n0_ref[0]
        m_s[...] = m0_ref[0]

    row, col = _iota2((L, L), 0), _iota2((L, L), 1)
    eye, tril = row == col, col <= row
    qs = [q_ref[0, :, hd * DK:(hd + 1) * DK] * (DK ** -0.5) for hd in heads]
    ks = [k_ref[0, :, hd * DK:(hd + 1) * DK] for hd in heads]
    vs = [v_ref[0, :, hd * DV:(hd + 1) * DV] for hd in heads]
    Cs = [C_s[hd * DK:(hd + 1) * DK, :] for hd in heads]
    qks = [_dot_nt(qs[hd], ks[hd]) for hd in heads]
    qCs = [_dot(qs[hd], Cs[hd]) for hd in heads]
    m_ts, scs, Ds, decs, w_cols, m_news = [], [], [], [], [], []
    for hd in heads:
        li_row = gi_ref[hd, pl.ds(c, 1), :] + bif_ref[0, hd]
        lf_row = _log_sigmoid(gf_ref[hd, pl.ds(c, 1), :] + bif_ref[1, hd])
        li_col, lf_col = _row_to_col(li_row, eye), _row_to_col(lf_row, eye)
        b_col = jnp.sum(jnp.where(tril, lf_row, 0.0), axis=1, keepdims=True)
        b_row = jnp.sum(jnp.where(row <= col, lf_col, 0.0), axis=0, keepdims=True)
        b_last = jnp.sum(lf_row, axis=1, keepdims=True)
        m_prev = m_s[hd:hd + 1, :]
        dlog = jnp.where(tril, b_col - b_row + li_row, -jnp.inf)
        inter = b_col + m_prev
        m_t = jnp.maximum(inter, jnp.max(dlog, axis=1, keepdims=True))
        m_new = jnp.max(jnp.where(row[:, :1] == L - 1, m_t, -jnp.inf), axis=0, keepdims=True)
        m_ts.append(m_t)
        scs.append(jnp.exp(inter - m_t))
        Ds.append(jnp.exp(dlog - m_t))
        decs.append(jnp.exp(b_last + m_prev - m_new))
        w_cols.append(jnp.exp(b_last - b_col + li_col - m_new))
        m_news.append(m_new)
    s_qks = [qks[hd] * Ds[hd] for hd in heads]
    nums = [_dot(s_qks[hd], vs[hd]) + scs[hd] * qCs[hd] for hd in heads]
    kws = [ks[hd] * w_cols[hd] for hd in heads]
    C_new = jnp.concatenate([decs[hd] * Cs[hd] + _dot_tn(kws[hd], vs[hd]) for hd in heads], axis=0)
    row8 = _iota2((H, 1), 0)
    n_new, m_new, outs = jnp.zeros((H, DK), F32), jnp.zeros((H, 1), F32), []
    for hd in heads:
        n = n_s[hd:hd + 1, :]
        den = jnp.sum(s_qks[hd], axis=1, keepdims=True) + scs[hd] * jnp.sum(qs[hd] * n, axis=1, keepdims=True)
        hc = nums[hd] / jnp.maximum(jnp.abs(den), jnp.exp(-m_ts[hd]))
        hn = hc * lax.rsqrt(jnp.mean(hc * hc, axis=-1, keepdims=True) + EPS) * gn_ref[hd:hd + 1, :]
        outs.append((hn * _sigmoid(og_ref[0, :, hd * DV:(hd + 1) * DV])).astype(hs_ref.dtype))
        n_new = n_new + jnp.where(row8 == hd, decs[hd] * n + jnp.sum(kws[hd], axis=0, keepdims=True), 0.0)
        m_new = m_new + jnp.where(row8 == hd, m_news[hd], 0.0)
    C_s[...] = C_new
    n_s[...] = n_new
    m_s[...] = m_new
    hs_ref[0] = jnp.concatenate(outs, axis=1)

    @pl.when(c == nC - 1)
    def _():
        c_out[0] = C_new.reshape(H, DK, DV)
        n_out[0] = n_new
        m_out[0] = m_new


def _mlstm(h, C0, n0, m0, w):
    B, T, D = h.shape
    H, DK, DV = A_HEADS, A_DK, A_DV
    h2 = h.reshape(B * T, D)
    n_main = 2 * H * DK + 2 * H * DV
    proj = _linear(h2, w['mlstm_w_in'], ncols=n_main).reshape(B, T, n_main)
    w_gate = jnp.pad(w['mlstm_w_in'][:, n_main:], ((0, 0), (0, LANE - 2 * H)))
    gates = _linear(h2, w_gate)
    L = math.gcd(T, A_CHUNK)
    nC = T // L
    to_rows = lambda a: a.reshape(B, nC, L, H).transpose(0, 3, 1, 2).reshape(B * H, nC, L)
    gi, gf = to_rows(gates[:, :H]), to_rows(gates[:, H:2 * H])
    wk, wv = H * DK, H * DV
    cols = lambda wd, off: pl.BlockSpec((1, L, wd), lambda b, c: (b, c, off))
    gs = pl.BlockSpec((H, nC, L), lambda b, c: (b, 0, 0))
    cs = pl.BlockSpec((1, H, DK, DV), lambda b, c: (b, 0, 0, 0))
    ns = pl.BlockSpec((1, H, DK), lambda b, c: (b, 0, 0))
    ms = pl.BlockSpec((1, H, 1), lambda b, c: (b, 0, 0))
    hs, C, n, m = pl.pallas_call(
        functools.partial(_mlstm_kernel, L=L, nC=nC), grid=(B, nC),
        in_specs=[pl.BlockSpec(memory_space=pltpu.SMEM), cols(wk, 0), cols(wk, 1), cols(wv, 2 * wk // wv),
                  cols(wv, 2 * wk // wv + 1), gs, gs, cs, ns, ms, pl.BlockSpec((H, DV), lambda b, c: (0, 0))],
        out_specs=[cols(wv, 0), cs, ns, ms],
        out_shape=[jax.ShapeDtypeStruct((B, T, H * DV), BF16), jax.ShapeDtypeStruct((B, H, DK, DV), F32),
                   jax.ShapeDtypeStruct((B, H, DK), F32), jax.ShapeDtypeStruct((B, H, 1), F32)],
        scratch_shapes=[pltpu.VMEM((H * DK, DV), F32), pltpu.VMEM((H, DK), F32), pltpu.VMEM((H, 1), F32)],
        compiler_params=_cp("arbitrary", "arbitrary"), name="mlstm",
    )(w['mlstm_b_if'], proj, proj, proj, proj, gi, gf, C0.astype(F32), n0.astype(F32),
      m0.astype(F32).reshape(B, H, 1), w['mlstm_norm_g'].astype(F32))
    y = _linear(hs.reshape(B * T, H * DV), w['mlstm_w_out'])
    return y.reshape(B, T, D), (C, n, m.reshape(B, H))


def _sb_weights(zs, valid, laters, upper):
    lss = [_log_sigmoid(z) for z in zs]
    l1s = [ls - z for ls, z in zip(lss, zs)]
    if valid is not None:
        l1s = [jnp.where(valid, l1, 0.0) for l1 in l1s]
    his = [l1.astype(BF16) for l1 in l1s]
    los = [(l1 - hi.astype(F32)).astype(BF16) for l1, hi in zip(l1s, his)]
    afters = [jnp.dot(hi, upper, preferred_element_type=F32) + jnp.dot(lo, upper, preferred_element_type=F32)
              for hi, lo in zip(his, los)]
    as_ = [jnp.exp(ls + after + later) for ls, after, later in zip(lss, afters, laters)]
    if valid is not None:
        as_ = [jnp.where(valid, a, 0.0) for a in as_]
    return as_, [later + jnp.sum(l1, axis=1, keepdims=True) for later, l1 in zip(laters, l1s)]


def _sb_kernel(bias_ref, q_ref, k_ref, v_ref, o_ref, *, TQ, HG):
    hg, i = pl.program_id(1), pl.program_id(2)
    DH = B_DH
    heads = range(HG)
    cols = lambda hd: slice(hd * DH, (hd + 1) * DH)
    qs = [q_ref[0, :, cols(hd)] * (DH ** -0.5) for hd in heads]
    biases = [bias_ref[hg * HG + hd] for hd in heads]
    row, col = _iota2((TQ, TQ), 0), _iota2((TQ, TQ), 1)
    upper = jnp.where(row > col, 1.0, 0.0).astype(BF16)

    def body(kk, carry):
        outs, laters = carry
        keys = pl.ds(pl.multiple_of((i - kk) * TQ, TQ), TQ)
        valid = col < row + kk * TQ
        zs = [_dot_nt(qs[hd], k_ref[0, keys, cols(hd)]) + biases[hd] for hd in heads]
        as_, laters = _sb_weights(zs, valid, laters, upper)
        return [outs[hd] + _dot(as_[hd], v_ref[0, keys, cols(hd)]) for hd in heads], laters

    init = ([jnp.zeros((TQ, DH), F32) for _ in heads], [jnp.zeros((TQ, 1), F32) for _ in heads])
    outs, _ = lax.fori_loop(0, i + 1, body, init)
    o_ref[0] = jnp.concatenate(outs, axis=1).astype(o_ref.dtype)


def _sb_prompt(proj, bias):
    B, T, _ = proj.shape
    H, HG = B_HEADS, SB_HEADS_PER_STEP
    TQ = math.gcd(T, SB_TILE)
    wd = HG * B_DH
    kvs = lambda off: pl.BlockSpec((1, T, wd), lambda b, hg, i: (b, 0, off + hg))
    qo = pl.BlockSpec((1, TQ, wd), lambda b, hg, i: (b, i, hg))
    return pl.pallas_call(
        functools.partial(_sb_kernel, TQ=TQ, HG=HG), grid=(B, H // HG, T // TQ),
        in_specs=[pl.BlockSpec(memory_space=pltpu.SMEM), qo, kvs(H // HG), kvs(2 * H // HG)],
        out_specs=qo,
        out_shape=jax.ShapeDtypeStruct((B, T, H * B_DH), BF16),
        compiler_params=_cp("arbitrary", "arbitrary", "arbitrary"), name="sb_prompt",
    )(bias, proj, proj, proj)


def _sb_dec_kernel(pt_ref, q_ref, bias_ref, kvn_ref, cache_hbm, o_ref, kv_buf, sem, acc_s, later_s,
                   *, n_seq, n_pages, TN):
    b, p = pl.program_id(0), pl.program_id(1)
    step = b * n_pages + p
    slot = lax.rem(step, 2)
    H = B_HEADS
    R = H * TN

    def page_copies(bb, pp, sl):
        page = pt_ref[bb, n_pages - 1 - pp]
        return [pltpu.make_async_copy(cache_hbm.at[page, :, kv, hd, :], kv_buf.at[sl, kv * H + hd], sem.at[sl])
                for kv in range(2) for hd in range(H)]

    @pl.when(step == 0)
    def _():
        for cp in page_copies(0, 0, 0):
            cp.start()

    @pl.when(step + 1 < n_seq * n_pages)
    def _():
        wrap = p + 1 == n_pages
        for cp in page_copies(jnp.where(wrap, b + 1, b), jnp.where(wrap, 0, p + 1), 1 - slot):
            cp.start()

    bias = bias_ref[...]
    row, col = _iota2((PAGE_SIZE, PAGE_SIZE), 0), _iota2((PAGE_SIZE, PAGE_SIZE), 1)
    upper = jnp.where(row > col, 1.0, 0.0).astype(BF16)

    def segment(rows_of, valid, later):
        z = jnp.concatenate([_dot_nt(q_ref[0, hd], rows_of(hd)) for hd in range(H)], axis=0) + bias
        (a,), (later,) = _sb_weights([z], valid, [later], upper)
        out = jnp.concatenate([_dot(a[hd * TN:(hd + 1) * TN], rows_of(H + hd)) for hd in range(H)], axis=0)
        return out, later

    @pl.when(p == 0)
    def _():
        rq, ck = _iota2((R, PAGE_SIZE), 0), _iota2((R, PAGE_SIZE), 1)
        out, later = segment(lambda i: kvn_ref[0, i], ck < lax.rem(rq, TN), jnp.zeros((R, 1), F32))
        acc_s[...] = out
        later_s[...] = later

    for cp in page_copies(b, p, slot):
        cp.wait()
    out, later = segment(lambda i: kv_buf[slot, i], None, later_s[...])
    acc_s[...] += out
    later_s[...] = later

    @pl.when(p == n_pages - 1)
    def _():
        o_ref[0] = acc_s[...]


def _sb_decode(proj, cache, page_table, bias):
    B, TN, _ = proj.shape
    H, DH = B_HEADS, B_DH
    HD = H * DH
    R = H * TN
    n_pages = page_table.shape[1]
    q = proj[:, :, :HD].reshape(B, TN, H, DH).transpose(0, 2, 1, 3) * (DH ** -0.5)
    kvn = proj[:, :, HD:].reshape(B, TN, 2 * H, DH).transpose(0, 2, 1, 3)
    kvn = jnp.pad(kvn, ((0, 0), (0, 0), (0, PAGE_SIZE - TN), (0, 0)))
    bias_col = jnp.repeat(bias.astype(F32), TN).reshape(R, 1)
    grid_spec = pltpu.PrefetchScalarGridSpec(
        num_scalar_prefetch=1, grid=(B, n_pages),
        in_specs=[pl.BlockSpec((1, H, TN, DH), lambda b, p, pt: (b, 0, 0, 0)),
                  pl.BlockSpec((R, 1), lambda b, p, pt: (0, 0)),
                  pl.BlockSpec((1, 2 * H, PAGE_SIZE, DH), lambda b, p, pt: (b, 0, 0, 0)),
                  pl.BlockSpec(memory_space=pl.ANY)],
        out_specs=pl.BlockSpec((1, R, DH), lambda b, p, pt: (b, 0, 0)),
        scratch_shapes=[pltpu.VMEM((2, 2 * H, PAGE_SIZE, DH), F32), pltpu.SemaphoreType.DMA((2,)),
                        pltpu.VMEM((R, DH), F32), pltpu.VMEM((R, 1), F32)])
    o = pl.pallas_call(
        functools.partial(_sb_dec_kernel, n_seq=B, n_pages=n_pages, TN=TN), grid_spec=grid_spec,
        out_shape=jax.ShapeDtypeStruct((B, R, DH), F32),
        compiler_params=_cp("arbitrary", "arbitrary"), name="sb_decode",
    )(page_table, q, bias_col, kvn, cache)
    return o.reshape(B, H, TN, DH).transpose(0, 2, 1, 3).reshape(B, TN, HD).astype(BF16)


def _sb_mixer(h, past, w):
    B, T, D = h.shape
    proj = _linear(h.reshape(B * T, D), w['sb_w_qkv']).reshape(B, T, 3 * B_HEADS * B_DH)
    if past is None:
        o = _sb_prompt(proj, w['sb_bias'])
    else:
        o = _sb_decode(proj, past[0], past[1], w['sb_bias'])
    y = _linear(o.reshape(B * T, B_HEADS * B_DH), w['sb_w_out'])
    kv_new = proj[:, :, B_HEADS * B_DH:].reshape(B, T, 2, B_HEADS, B_DH)
    return y.reshape(B, T, D), kv_new


def _t5_bucket(dist):
    max_exact = N_BUCKETS // 2
    large = max_exact + (jnp.log(jnp.maximum(dist, 1).astype(F32) / max_exact)
                         / math.log(MAX_DISTANCE / max_exact) * (N_BUCKETS - max_exact)).astype(jnp.int32)
    return jnp.where(dist < max_exact, dist, jnp.minimum(large, N_BUCKETS - 1))


def _tap_bias(rel_bias, g, taps, valid):
    win, dil = C_GROUPS[g]
    J = win // dil + 1
    valid = jnp.logical_and(valid, jnp.logical_and(taps >= 0, taps < J))
    tab = rel_bias[_t5_bucket(dil * jnp.arange(J))][:, g * C_HPG:(g + 1) * C_HPG].astype(F32)
    hit = taps[None, ..., None] == jnp.arange(J)
    vals = jnp.sum(jnp.where(hit, tab.T.reshape((C_HPG,) + (1,) * taps.ndim + (J,)), 0.0), axis=-1)
    return jnp.where(valid[None], vals, -jnp.inf)


def _band_kernel(q_ref, kp_ref, kc_ref, vp_ref, vc_ref, bias_ref, o_ref, lse_ref, *, TQ):
    i = pl.program_id(2)
    H, DH = C_HPG, C_DH
    heads = range(H)
    head = lambda ref, hd: ref[0, :, hd * DH:(hd + 1) * DH]
    qs = [head(q_ref, hd) * (DH ** -0.5) for hd in heads]
    s_cs = [_dot_nt(qs[hd], head(kc_ref, hd)) + bias_ref[hd, :, TQ:] for hd in heads]
    s_ps = [jnp.where(i > 0, _dot_nt(qs[hd], head(kp_ref, hd)) + bias_ref[hd, :, :TQ], -jnp.inf) for hd in heads]
    mxs = [jnp.maximum(jnp.max(s_cs[hd], axis=1, keepdims=True), jnp.max(s_ps[hd], axis=1, keepdims=True))
           for hd in heads]
    p_cs = [jnp.exp(s_cs[hd] - mxs[hd]) for hd in heads]
    p_ps = [jnp.exp(s_ps[hd] - mxs[hd]) for hd in heads]
    ls = [jnp.sum(p_cs[hd], axis=1, keepdims=True) + jnp.sum(p_ps[hd], axis=1, keepdims=True) for hd in heads]
    outs = [(_dot(p_cs[hd], head(vc_ref, hd)) + _dot(p_ps[hd], head(vp_ref, hd))) / ls[hd] for hd in heads]
    o_ref[0] = jnp.concatenate(outs, axis=1)
    lane = _iota2((TQ, H), 1)
    lse = jnp.zeros((TQ, H), F32)
    for hd in heads:
        lse = lse + jnp.where(lane == hd, mxs[hd] + jnp.log(ls[hd]), 0.0)
    lse_ref[0, 0] = lse


def _dw_prompt_group(proj, rel_bias, g):
    B, T, W3 = proj.shape
    win, dil = C_GROUPS[g]
    H, DH = C_HPG, C_DH
    TQ = win // dil
    Ts = T // dil
    assert Ts % TQ == 0
    HD = H * DH
    nb = 3
    pv = proj.reshape(B, T, nb, C_NG, HD)[:, :, :, g].reshape(B, Ts, dil * nb * HD)
    t_loc, s_loc = jnp.arange(TQ)[:, None], jnp.arange(2 * TQ)[None, :] - TQ
    bias = _tap_bias(rel_bias, g, t_loc - s_loc, jnp.ones((TQ, 2 * TQ), bool))
    blk = lambda which, prev: pl.BlockSpec(
        (1, TQ, HD), lambda b, r, i: (b, jnp.maximum(i - 1, 0) if prev else i, r * nb + which))
    o, lse = pl.pallas_call(
        functools.partial(_band_kernel, TQ=TQ), grid=(B, dil, Ts // TQ),
        in_specs=[blk(0, False), blk(1, True), blk(1, False), blk(2, True), blk(2, False),
                  pl.BlockSpec((H, TQ, 2 * TQ), lambda b, r, i: (0, 0, 0))],
        out_specs=[pl.BlockSpec((1, TQ, HD), lambda b, r, i: (b, i, r)),
                   pl.BlockSpec((1, 1, TQ, H), lambda b, r, i: (b, r, i, 0))],
        out_shape=[jax.ShapeDtypeStruct((B, Ts, dil * HD), F32),
                   jax.ShapeDtypeStruct((B, dil, Ts, H), F32)],
        compiler_params=_cp("arbitrary", "arbitrary", "arbitrary"), name="dw_band",
    )(pv, pv, pv, pv, pv, bias)
    lse = lse.transpose(0, 2, 1, 3).reshape(B, T, H)
    return o.reshape(B, T, H * DH), lse


def _dw_dec_kernel(q_ref, bmn_ref, bm_ref, kvn_ref, buf_ref, o_ref, lse_ref, m_s, l_s, acc_s, *, n_tiles, TN):
    wi = pl.program_id(1)
    H = C_HPG
    R = H * TN

    def segment(src_ref, bm, m_old, l_old, acc_old):
        n_keys = bm.shape[1]
        head_rows = lambda first: src_ref[pl.ds(0, 1), pl.ds(first, n_keys, stride=2 * H), :][0]
        s = jnp.concatenate([_dot_nt(q_ref[0, hd], head_rows(hd)) for hd in range(H)], axis=0) + bm
        m_new = jnp.maximum(m_old, jnp.max(s, axis=1, keepdims=True))
        alpha = jnp.exp(m_old - m_new)
        p = jnp.exp(s - m_new)
        pv = jnp.concatenate([_dot(p[hd * TN:(hd + 1) * TN], head_rows(H + hd)) for hd in range(H)], axis=0)
        m_s[...] = m_new
        l_s[...] = alpha * l_old + jnp.sum(p, axis=1, keepdims=True)
        acc_s[...] = alpha * acc_old + pv

    @pl.when(wi == 0)
    def _():
        segment(kvn_ref, bmn_ref[...], jnp.full((R, 1), NEG_BIG, F32), jnp.zeros((R, 1), F32),
                jnp.zeros((R, C_DH), F32))

    segment(buf_ref, bm_ref[...], m_s[...], l_s[...], acc_s[...])

    @pl.when(wi == n_tiles - 1)
    def _():
        l = l_s[...]
        lse_ref[0] = m_s[...] + jnp.log(l)
        o_ref[0] = acc_s[...] / l


def _dw_decode_group(proj, buf, rel_bias, g):
    B, TN, _ = proj.shape
    win, dil = C_GROUPS[g]
    H, DH = C_HPG, C_DH
    HD = H * DH
    R = H * TN
    W = buf.shape[1]
    TW = min(W, 512)
    p6 = proj.reshape(B, TN, 3, C_NG, H, DH)
    q = p6[:, :, 0, g].transpose(0, 2, 1, 3) * (DH ** -0.5)
    kvn = jnp.pad(p6[:, :, 1:, g].reshape(B, TN * 2 * H, DH), ((0, 0), (0, (LANE - TN) * 2 * H), (0, 0)))
    t = jnp.arange(TN)[:, None]
    dist_buf = W + t - jnp.arange(W)[None, :]
    dist_new = t - jnp.arange(LANE)[None, :]
    bias_of = lambda dist, ok: _tap_bias(rel_bias, g, dist // dil, jnp.logical_and(ok, dist % dil == 0))
    bm = bias_of(dist_buf, jnp.ones_like(dist_buf, bool)).reshape(R, W)
    bmn = bias_of(dist_new, jnp.arange(LANE)[None, :] < TN).reshape(R, LANE)
    o, lse = pl.pallas_call(
        functools.partial(_dw_dec_kernel, n_tiles=W // TW, TN=TN), grid=(B, W // TW),
        in_specs=[pl.BlockSpec((1, H, TN, DH), lambda b, wi: (b, 0, 0, 0)),
                  pl.BlockSpec((R, LANE), lambda b, wi: (0, 0)),
                  pl.BlockSpec((R, TW), lambda b, wi: (0, wi)),
                  pl.BlockSpec((1, LANE * 2 * H, DH), lambda b, wi: (b, 0, 0)),
                  pl.BlockSpec((1, TW * 2 * H, DH), lambda b, wi: (b, wi, 0))],
        out_specs=[pl.BlockSpec((1, R, DH), lambda b, wi: (b, 0, 0)),
                   pl.BlockSpec((1, R, 1), lambda b, wi: (b, 0, 0))],
        out_shape=[jax.ShapeDtypeStruct((B, R, DH), F32), jax.ShapeDtypeStruct((B, R, 1), F32)],
        scratch_shapes=[pltpu.VMEM((R, 1), F32), pltpu.VMEM((R, 1), F32), pltpu.VMEM((R, DH), F32)],
        compiler_params=_cp("arbitrary", "arbitrary"), name="dw_decode",
    )(q, bmn, bm, kvn, buf.reshape(B, W * 2 * H, DH))
    o = o.reshape(B, H, TN, DH).transpose(0, 2, 1, 3).reshape(B, TN, HD)
    lse = lse.reshape(B, H, TN).transpose(0, 2, 1)
    return o, lse


def _dw_combine_kernel(o0, o1, o2, l0, l1, l2, out_ref):
    ls = [l0[0], l1[0], l2[0]]
    mx = jnp.maximum(jnp.maximum(ls[0], ls[1]), ls[2])
    es = [jnp.exp(l - mx) for l in ls]
    tot = es[0] + es[1] + es[2]
    ws = [e / tot for e in es]
    for hd in range(C_HPG):
        cols = slice(hd * C_DH, (hd + 1) * C_DH)
        acc = ws[0][:, hd:hd + 1] * o0[0, :, cols]
        acc = acc + ws[1][:, hd:hd + 1] * o1[0, :, cols]
        acc = acc + ws[2][:, hd:hd + 1] * o2[0, :, cols]
        out_ref[0, :, cols] = acc.astype(out_ref.dtype)


def _dw_combine(outs, lses):
    B, T, HD = outs[0].shape
    tt = min(T, 256)
    os_ = pl.BlockSpec((1, tt, HD), lambda b, t: (b, t, 0))
    ls_ = pl.BlockSpec((1, tt, C_HPG), lambda b, t: (b, t, 0))
    return pl.pallas_call(
        _dw_combine_kernel, grid=(B, T // tt), in_specs=[os_] * 3 + [ls_] * 3, out_specs=os_,
        out_shape=jax.ShapeDtypeStruct((B, T, HD), BF16),
        compiler_params=_cp("arbitrary", "arbitrary"), name="dw_combine",
    )(*outs, *lses)


def _dw_mixer(h, bufs, w):
    B, T, D = h.shape
    W3 = 3 * C_NG * C_HPG * C_DH
    proj = _linear(h.reshape(B * T, D), w['dw_w_qkv']).reshape(B, T, W3)
    p6 = proj.reshape(B, T, 3, C_NG, C_HPG, C_DH)
    outs, lses, new_bufs = [], [], []
    for g, (win, dil) in enumerate(C_GROUPS):
        kv_g = p6[:, :, 1:, g]
        if bufs is None:
            o, lse = _dw_prompt_group(proj, w['rel_bias'], g)
            new_bufs.append(kv_g[:, T - min(win, T):])
        else:
            o, lse = _dw_decode_group(proj, bufs[g], w['rel_bias'], g)
            new_bufs.append(jnp.concatenate([bufs[g].astype(F32), kv_g], axis=1)[:, T:])
        outs.append(o)
        lses.append(lse)
    o = _dw_combine(outs, lses)
    y = _linear(o.reshape(B * T, C_HPG * C_DH), w['dw_w_out'])
    return y.reshape(B, T, D), tuple(new_bufs)


def _conv_kernel(x_ref, halo_ref, w_ref, o_ref):
    x = x_ref[0]
    halo = halo_ref[0, 0]
    tt = x.shape[0]
    head = x[:8]
    r8 = _iota2(head.shape, 0)
    acc = x * w_ref[D_CONV - 1:D_CONV, :]
    for s in range(1, D_CONV):
        top = jnp.where(r8 < s, pltpu.roll(halo, s, axis=0), pltpu.roll(head, s, axis=0))
        if tt > 8:
            shifted = jnp.concatenate([top, pltpu.roll(x, s, axis=0)[8:]], axis=0)
        else:
            shifted = top
        acc = acc + shifted * w_ref[D_CONV - 1 - s:D_CONV - s, :]
    o_ref[0] = acc * _sigmoid(acc)


def _gdn_conv(proj, conv_buf, conv_w):
    B, T, _ = proj.shape
    C = D_CONV_CH
    tt, tc = min(T, 256), 1024
    nT = T // tt
    first = jnp.pad(conv_buf.astype(F32), ((0, 0), (8 - (D_CONV - 1), 0), (0, 0)))[:, None]
    if nT > 1:
        tails = proj[:, :, :C].reshape(B, nT, tt, C)[:, :-1, tt - 8:]
        halo = jnp.concatenate([first, tails], axis=1)
    else:
        halo = first
    return pl.pallas_call(
        _conv_kernel, grid=(B, nT, C // tc),
        in_specs=[pl.BlockSpec((1, tt, tc), lambda b, t, c: (b, t, c)),
                  pl.BlockSpec((1, 1, 8, tc), lambda b, t, c: (b, t, 0, c)),
                  pl.BlockSpec((D_CONV, tc), lambda b, t, c: (0, c))],
        out_specs=pl.BlockSpec((1, tt, tc), lambda b, t, c: (b, t, c)),
        out_shape=jax.ShapeDtypeStruct((B, T, C), F32),
        compiler_params=_cp("arbitrary", "arbitrary", "arbitrary"), name="gdn_conv",
    )(proj, halo, conv_w.astype(F32))


def _gdn_kernel(par_ref, q_ref, k_ref, v_ref, z_ref, braw_ref, araw_ref, s0_ref, gn_ref, o_ref, s_out, S_s,
                *, L, nC, HG, rep):
    hg, c = pl.program_id(1), pl.program_id(2)
    DK, DV = D_DK, D_DV

    @pl.when(c == 0)
    def _():
        S_s[...] = s0_ref[0].reshape(HG * DK, DV)

    row, col = _iota2((L, L), 0), _iota2((L, L), 1)
    eye, tril = row == col, col <= row
    ident = jnp.where(eye, 1.0, 0.0)
    heads = range(HG)
    qs, ks, kks, qks = [], [], [], []
    for jq in range(HG // rep):
        q, k = q_ref[0, :, jq * DK:(jq + 1) * DK], k_ref[0, :, jq * DK:(jq + 1) * DK]
        q = q * lax.rsqrt(jnp.sum(q * q, axis=-1, keepdims=True) + EPS) * (DK ** -0.5)
        k = k * lax.rsqrt(jnp.sum(k * k, axis=-1, keepdims=True) + EPS)
        qs += [q] * rep
        ks += [k] * rep
        kks += [_dot_nt(k, k)] * rep
        qks += [_dot_nt(q, k)] * rep
    betas, Gs, GLs, decays, Ns = [], [], [], [], []
    for j in heads:
        hd = hg * HG + j
        beta_row = _sigmoid(braw_ref[j, pl.ds(c, 1), :])
        g_row = -jnp.exp(par_ref[0, hd]) * _softplus(araw_ref[j, pl.ds(c, 1), :] + par_ref[1, hd])
        beta_col, g_col = _row_to_col(beta_row, eye), _row_to_col(g_row, eye)
        G_col = jnp.sum(jnp.where(tril, g_row, 0.0), axis=1, keepdims=True)
        G_row = jnp.sum(jnp.where(row <= col, g_col, 0.0), axis=0, keepdims=True)
        decay = jnp.exp(jnp.where(tril, G_col - G_row, -jnp.inf))
        betas.append(beta_col)
        Gs.append(G_col)
        GLs.append(jnp.sum(g_row, axis=1, keepdims=True))
        decays.append(decay)
        Ns.append(jnp.where(col < row, -(beta_col * kks[j]) * decay, 0.0))
    invs = [ident + N for N in Ns]
    for _ in range(int(math.log2(L)) - 1):
        Ns = [_dot_hi(N, N) for N in Ns]
        invs = [inv + _dot_hi(inv, N) for inv, N in zip(invs, Ns)]
    eGs = [jnp.exp(G) for G in Gs]
    Us = [_dot_hi(invs[j], v_ref[0, :, j * DV:(j + 1) * DV] * betas[j]) for j in heads]
    Ws = [_dot_hi(invs[j], ks[j] * (betas[j] * eGs[j])) for j in heads]
    Ss = [S_s[j * DK:(j + 1) * DK, :] for j in heads]
    v_news = [Us[j] - _dot(Ws[j], Ss[j]) for j in heads]
    os_ = [_dot(qs[j] * eGs[j], Ss[j]) + _dot(qks[j] * decays[j], v_news[j]) for j in heads]
    states = [jnp.exp(GLs[j]) * Ss[j] + _dot_tn(ks[j] * jnp.exp(GLs[j] - Gs[j]), v_news[j]) for j in heads]
    outs = []
    for j in heads:
        o, z = os_[j], z_ref[0, :, j * DV:(j + 1) * DV]
        on = o * lax.rsqrt(jnp.mean(o * o, axis=-1, keepdims=True) + EPS) * gn_ref[...]
        outs.append((on * (z * _sigmoid(z))).astype(o_ref.dtype))
    S_new = jnp.concatenate(states, axis=0)
    S_s[...] = S_new
    o_ref[0] = jnp.concatenate(outs, axis=1)

    @pl.when(c == nC - 1)
    def _():
        s_out[0] = S_new.reshape(HG, DK, DV)


def _gdn_mixer(h, conv_buf, S0, w):
    B, T, D = h.shape
    HQ, HV, DK, DV = D_QK_HEADS, D_V_HEADS, D_DK, D_DV
    C = D_CONV_CH
    n_v = HV * DV
    n_main = C + n_v
    h2 = h.reshape(B * T, D)
    proj = _linear(h2, w['gdn_w_in'], ncols=n_main).reshape(B, T, n_main)
    w_ba = jnp.pad(w['gdn_w_in'][:, n_main:], ((0, 0), (0, LANE - 2 * HV)))
    ba = _linear(h2, w_ba)
    conv = _gdn_conv(proj, conv_buf, w['gdn_conv_w'])
    new_buf = jnp.concatenate([conv_buf.astype(F32), proj[:, :, :C]], axis=1)[:, T:] if T < D_CONV - 1 \
        else proj[:, T - (D_CONV - 1):, :C]
    L = math.gcd(T, D_CHUNK)
    nC = T // L
    to_rows = lambda a: a.reshape(B, nC, L, HV).transpose(0, 3, 1, 2).reshape(B * HV, nC, L)
    braw, araw = to_rows(ba[:, :HV]), to_rows(ba[:, HV:2 * HV])
    par = jnp.stack([w['gdn_A_log'], w['gdn_dt_bias']]).astype(F32)
    rep = HV // HQ
    HG = 8
    wq, wv = HG // rep * DK, HG * DV
    blk = lambda wd, off: pl.BlockSpec((1, L, wd), lambda b, hg, c: (b, c, off + hg))
    gs = pl.BlockSpec((HG, nC, L), lambda b, hg, c: (b * (HV // HG) + hg, 0, 0))
    ss = pl.BlockSpec((1, HG, DK, DV), lambda b, hg, c: (b, hg, 0, 0))
    o, S = pl.pallas_call(
        functools.partial(_gdn_kernel, L=L, nC=nC, HG=HG, rep=rep), grid=(B, HV // HG, nC),
        in_specs=[pl.BlockSpec(memory_space=pltpu.SMEM),
                  blk(wq, 0), blk(wq, HQ * DK // wq), blk(wv, 2 * HQ * DK // wv), blk(wv, C // wv), gs, gs, ss,
                  pl.BlockSpec((1, DV), lambda b, hg, c: (0, 0))],
        out_specs=[blk(wv, 0), ss],
        out_shape=[jax.ShapeDtypeStruct((B, T, n_v), BF16), jax.ShapeDtypeStruct((B, HV, DK, DV), F32)],
        scratch_shapes=[pltpu.VMEM((HG * DK, DV), F32)],
        compiler_params=_cp("arbitrary", "arbitrary", "arbitrary"), name="gdn",
    )(par, conv, conv, conv, proj, braw, araw, S0.astype(F32), w['gdn_norm_g'].reshape(1, DV).astype(F32))
    y = _linear(o.reshape(B * T, n_v), w['gdn_w_out'])
    return y.reshape(B, T, D), (new_buf, S)


def _ffn(h, w, i):
    B, T, D = h.shape
    act = _swiglu_up(h.reshape(B * T, D), w['ffn_w_gu'], (i,), D_FF)
    return _linear(act, w['ffn_w_down'], sel=(i,)).reshape(B, T, D)


def _moe_plan(route, n_tiles):
    E, TM = N_EXPERTS, MOE_TM
    sel1, sel2 = route[:, :E], route[:, E:2 * E]
    cnt1 = jnp.sum(sel1, axis=0)
    cnt = cnt1 + jnp.sum(sel2, axis=0)
    pcnt = jnp.ceil(cnt / TM) * TM
    pend = jnp.cumsum(pcnt)
    pstart = pend - pcnt
    rank1 = jnp.cumsum(sel1, axis=0) - sel1
    rank2 = cnt1[None] + jnp.cumsum(sel2, axis=0) - sel2
    dest1 = jnp.sum(sel1 * (pstart[None] + rank1), axis=1)
    dest2 = jnp.sum(sel2 * (pstart[None] + rank2), axis=1)
    dest = jnp.concatenate([dest1, dest2]).astype(jnp.int32)
    n_used = (pend[-1] / TM).astype(jnp.int32)
    first_row = jnp.minimum(jnp.arange(n_tiles), n_used - 1).astype(F32) * TM
    tile_expert = jnp.minimum(jnp.sum((first_row[:, None] >= pend[None, :]).astype(jnp.int32), axis=1), E - 1)
    M = route.shape[0]
    token_of = (jnp.argsort(dest) % M).astype(jnp.int32)
    of_tile = (tile_expert[:, None] == jnp.arange(E)[None, :]).astype(F32)
    per_row = lambda v: jnp.repeat(jnp.sum(of_tile * v[None], axis=1), TM)
    rank = jnp.arange(n_tiles * TM).astype(F32) - per_row(pstart)
    packed = per_row(jnp.cumsum(cnt) - cnt) + rank
    src = jnp.where(jnp.logical_and(rank >= 0, rank < per_row(cnt)),
                    token_of[jnp.clip(packed, 0, TOP_K * M - 1).astype(jnp.int32)], 0)
    return dest, src, tile_expert, n_used.reshape(1)


def _row_copy(src_hbm, src_row, dst, dst_row, sem):
    return pltpu.make_async_copy(src_hbm.at[pl.ds(src_row, 1)], dst.at[pl.ds(dst_row, 1)], sem)


def _lagged_copies(n, copies_of, lag):
    def body(t, carry):
        for cp in copies_of(t):
            cp.start()

        @pl.when(t >= lag)
        def _():
            for cp in copies_of(t - lag):
                cp.wait()
        return carry

    def drain(t, carry):
        for cp in copies_of(t):
            cp.wait()
        return carry

    lax.fori_loop(0, n, body, 0)
    lax.fori_loop(max(n - lag, 0), n, drain, 0)


def _dispatch_kernel(src_ref, h_hbm, o_ref, buf, sem, *, TR):
    base = pl.program_id(0) * TR
    _lagged_copies(TR, lambda r: [_row_copy(h_hbm, src_ref[base + r], buf, r, sem)], DMA_LAG)
    o_ref[...] = buf[...].astype(o_ref.dtype)


def _dispatch(h, src):
    D = h.shape[1]
    P = src.shape[0]
    TR = MOE_TM
    grid_spec = pltpu.PrefetchScalarGridSpec(
        num_scalar_prefetch=1, grid=(P // TR,), in_specs=[pl.BlockSpec(memory_space=pl.ANY)],
        out_specs=pl.BlockSpec((TR, D), lambda t, src: (t, 0)),
        scratch_shapes=[pltpu.VMEM((TR, D), F32), pltpu.SemaphoreType.DMA(())])
    return pl.pallas_call(
        functools.partial(_dispatch_kernel, TR=TR), grid_spec=grid_spec,
        out_shape=jax.ShapeDtypeStruct((P, D), BF16),
        compiler_params=_cp("arbitrary"), name="moe_dispatch",
    )(src, h)


def _tile_is_new(te_ref, m):
    return jnp.logical_or(m == 0, te_ref[m] != te_ref[jnp.maximum(m - 1, 0)])


def _moe_up_kernel(te_ref, nu_ref, x_ref, wg_ref, wu_ref, o_ref, wg_bf, wu_bf):
    m = pl.program_id(1)

    @pl.when(_tile_is_new(te_ref, m))
    def _():
        wg_bf[...] = wg_ref[...].astype(BF16)
        wu_bf[...] = wu_ref[...].astype(BF16)

    @pl.when(m < nu_ref[0])
    def _():
        x = x_ref[...]
        g = jnp.dot(x, wg_bf[...], preferred_element_type=F32)
        u = jnp.dot(x, wu_bf[...], preferred_element_type=F32)
        o_ref[...] = (g * _sigmoid(g) * u).astype(o_ref.dtype)

    @pl.when(m >= nu_ref[0])
    def _():
        o_ref[...] = jnp.zeros_like(o_ref)


def _moe_down_kernel(te_ref, nu_ref, a_ref, w_ref, o_ref, w_bf):
    m = pl.program_id(1)

    @pl.when(_tile_is_new(te_ref, m))
    def _():
        w_bf[...] = w_ref[...].astype(BF16)

    @pl.when(m < nu_ref[0])
    def _():
        o_ref[...] = jnp.dot(a_ref[...], w_bf[...], preferred_element_type=F32)

    @pl.when(m >= nu_ref[0])
    def _():
        o_ref[...] = jnp.zeros_like(o_ref)


def _moe_experts(xg, tile_expert, n_used, w, i):
    P, K = xg.shape
    F, TM = D_FF_EXPERT, MOE_TM
    NT = P // TM
    tn = 512
    nb = F // tn
    row = lambda j, m, te, nu: (jnp.minimum(m, nu[0] - 1), 0)
    out = lambda j, m, te, nu: (m, j)
    act = pl.pallas_call(
        _moe_up_kernel,
        grid_spec=pltpu.PrefetchScalarGridSpec(
            num_scalar_prefetch=2, grid=(nb, NT),
            in_specs=[pl.BlockSpec((TM, K), row),
                      pl.BlockSpec((None, None, K, tn), lambda j, m, te, nu: (i, te[m], 0, j)),
                      pl.BlockSpec((None, None, K, tn), lambda j, m, te, nu: (i, te[m], 0, j + nb))],
            out_specs=pl.BlockSpec((TM, tn), out),
            scratch_shapes=[pltpu.VMEM((K, tn), BF16), pltpu.VMEM((K, tn), BF16)]),
        out_shape=jax.ShapeDtypeStruct((P, F), BF16),
        compiler_params=_cp("arbitrary", "arbitrary"), name="moe_up",
    )(tile_expert, n_used, xg, w['moe_w_gu'], w['moe_w_gu'])
    tn = _pick_tn(F, K)
    return pl.pallas_call(
        _moe_down_kernel,
        grid_spec=pltpu.PrefetchScalarGridSpec(
            num_scalar_prefetch=2, grid=(K // tn, NT),
            in_specs=[pl.BlockSpec((TM, F), row),
                      pl.BlockSpec((None, None, F, tn), lambda j, m, te, nu: (i, te[m], 0, j))],
            out_specs=pl.BlockSpec((TM, tn), out),
            scratch_shapes=[pltpu.VMEM((F, tn), BF16)]),
        out_shape=jax.ShapeDtypeStruct((P, K), F32),
        compiler_params=_cp("arbitrary", "arbitrary"), name="moe_down",
    )(tile_expert, n_used, act, w['moe_w_down'])


def _combine_kernel(pos_ref, yg_hbm, route_ref, o_ref, buf, sem, *, M, TR):
    base = pl.program_id(0) * TR
    _lagged_copies(TR, lambda r: [_row_copy(yg_hbm, pos_ref[k * M + base + r], buf.at[k], r, sem.at[k])
                                  for k in range(TOP_K)], DMA_LAG)
    g1 = route_ref[:, 2 * N_EXPERTS:2 * N_EXPERTS + 1]
    g2 = route_ref[:, 2 * N_EXPERTS + 1:2 * N_EXPERTS + 2]
    o_ref[...] = g1 * buf[0] + g2 * buf[1]


def _combine(yg, pos, route):
    M = route.shape[0]
    D = yg.shape[1]
    TR = _pick_rows(M)
    grid_spec = pltpu.PrefetchScalarGridSpec(
        num_scalar_prefetch=1, grid=(M // TR,),
        in_specs=[pl.BlockSpec(memory_space=pl.ANY), pl.BlockSpec((TR, LANE), lambda t, pos: (t, 0))],
        out_specs=pl.BlockSpec((TR, D), lambda t, pos: (t, 0)),
        scratch_shapes=[pltpu.VMEM((TOP_K, TR, D), F32), pltpu.SemaphoreType.DMA((TOP_K,))])
    return pl.pallas_call(
        functools.partial(_combine_kernel, M=M, TR=TR), grid_spec=grid_spec,
        out_shape=jax.ShapeDtypeStruct((M, D), F32),
        compiler_params=_cp("arbitrary"), name="moe_combine",
    )(pos, yg, route)


def _moe(h, w, i):
    M, D = h.shape
    n_tiles = -(-(TOP_K * M + N_EXPERTS * (MOE_TM - 1)) // MOE_TM)
    w_r = jnp.pad(w['moe_router'][i], ((0, 0), (0, LANE - N_EXPERTS)))
    route = _router(h, w_r)
    dest, src, tile_expert, n_used = _moe_plan(route, n_tiles)
    xg = _dispatch(h, src)
    yg = _moe_experts(xg, tile_expert, n_used, w, i)
    return _combine(yg, dest, route)


def _trunk(xs, mods, pasts, w):
    S = range(len(xs))
    news = [{} for _ in S]
    depth = mods[0].shape[0]
    ng = w['norm_g']
    ffn_dtype = lambda layer: BF16 if layer % 2 == 0 else F32
    hs = [_norm(xs[s], ng[0, 0], nmod=mods[s][0], mod_rows=(0, 1)) for s in S]
    outs = [None for _ in S]
    for layer in range(depth):
        kind = layer % 4
        ys = []
        for s in S:
            h, past, new = hs[s], pasts[s], news[s]
            if kind == 0:
                y, new['mlstm'] = _mlstm(h, *past['mlstm'], w)
            elif kind == 1:
                y, new['sb'] = _sb_mixer(h, past['sb'], w)
            elif kind == 2:
                y, new['dw'] = _dw_mixer(h, past['dw'], w)
            else:
                y, new['gdn'] = _gdn_mixer(h, *past['gdn'], w)
            ys.append(y)
        for s in S:
            xs[s], hs[s] = _norm(xs[s], ng[layer, 1], y=ys[s], gmod=mods[s][layer], gate_row=2,
                                 nmod=mods[s][layer], mod_rows=(3, 4), h_dtype=ffn_dtype(layer))
        if layer % 2 == 0:
            ys = [_ffn(hs[s], w, layer // 2) for s in S]
        else:
            sizes = [hs[s].shape[0] * hs[s].shape[1] for s in S]
            y_all = _moe(jnp.concatenate([hs[s].reshape(sizes[s], D_MODEL) for s in S], axis=0), w, layer // 2)
            offs = np.cumsum([0] + sizes)
            ys = [y_all[offs[s]:offs[s + 1]].reshape(hs[s].shape) for s in S]
        for s in S:
            if layer + 1 < depth:
                xs[s], hs[s] = _norm(xs[s], ng[layer + 1, 0], y=ys[s], gmod=mods[s][layer], gate_row=5,
                                     nmod=mods[s][layer + 1], mod_rows=(0, 1))
            else:
                outs[s] = _norm(xs[s], w['final_g'], y=ys[s], gmod=mods[s][layer], gate_row=5, h_dtype=F32)
    return outs, news


def kernel(x_prompt, x_sample, c_prompt, c_sample, state_mlstm_C, state_mlstm_n, state_mlstm_m, cache_kv_sb, page_table, cache_kv_dw1, cache_kv_dw2, cache_kv_dw3, state_conv_gdn, state_S_gdn, w_ada, b_ada, norm_g, final_g, mlstm_w_in, mlstm_b_if, mlstm_norm_g, mlstm_w_out, sb_w_qkv, sb_w_out, sb_bias, dw_w_qkv, dw_w_out, rel_bias, gdn_w_in, gdn_conv_w, gdn_A_log, gdn_dt_bias, gdn_norm_g, gdn_w_out, ffn_w_gu, ffn_w_down, moe_router, moe_w_gu, moe_w_down):
    w = dict(norm_g=norm_g, final_g=final_g, mlstm_w_in=mlstm_w_in, mlstm_b_if=mlstm_b_if,
             mlstm_norm_g=mlstm_norm_g, mlstm_w_out=mlstm_w_out, sb_w_qkv=sb_w_qkv, sb_w_out=sb_w_out,
             sb_bias=sb_bias, dw_w_qkv=dw_w_qkv, dw_w_out=dw_w_out, rel_bias=rel_bias, gdn_w_in=gdn_w_in,
             gdn_conv_w=gdn_conv_w, gdn_A_log=gdn_A_log, gdn_dt_bias=gdn_dt_bias, gdn_norm_g=gdn_norm_g,
             gdn_w_out=gdn_w_out, ffn_w_gu=ffn_w_gu, ffn_w_down=ffn_w_down, moe_router=moe_router,
             moe_w_gu=moe_w_gu, moe_w_down=moe_w_down)
    Bp, Bd = x_prompt.shape[0], x_sample.shape[0]
    depth = w_ada.shape[0]
    rows = -(-(Bp + Bd) // 8) * 8
    c_all = jnp.pad(jnp.concatenate([c_prompt, c_sample], axis=0), ((0, rows - Bp - Bd), (0, 0)))
    mod = _ada(c_all, w_ada, b_ada).reshape(depth, rows, 6, D_MODEL)
    past_p = {
        'mlstm': (jnp.zeros((Bp, A_HEADS, A_DK, A_DV), F32), jnp.zeros((Bp, A_HEADS, A_DK), F32),
                  jnp.zeros((Bp, A_HEADS), F32)),
        'sb': None,
        'dw': None,
        'gdn': (jnp.zeros((Bp, D_CONV - 1, D_CONV_CH), F32), jnp.zeros((Bp, D_V_HEADS, D_DK, D_DV), F32)),
    }
    past_s = {
        'mlstm': (state_mlstm_C, state_mlstm_n, state_mlstm_m),
        'sb': (cache_kv_sb, page_table),
        'dw': (cache_kv_dw1, cache_kv_dw2, cache_kv_dw3),
        'gdn': (state_conv_gdn, state_S_gdn),
    }
    (y_prompt, y_sample), (new_p, new_s) = _trunk(
        [x_prompt, x_sample], [mod[:, :Bp], mod[:, Bp:Bp + Bd]], [past_p, past_s], w)
    C_p, n_p, m_p = new_p['mlstm']
    C_s, n_s, m_s = new_s['mlstm']
    dw1_p, dw2_p, dw3_p = new_p['dw']
    dw1_s, dw2_s, dw3_s = new_s['dw']
    conv_p, S_p = new_p['gdn']
    conv_s, S_s = new_s['gdn']
    return (y_prompt, y_sample, C_p, n_p, m_p, C_s, n_s, m_s, new_p['sb'], new_s['sb'],
            dw1_p, dw2_p, dw3_p, dw1_s, dw2_s, dw3_s, conv_p, S_p, conv_s, S_s)
```

```python
import functools
import math

import jax
import jax.numpy as jnp
import numpy as np
from jax import lax
from jax.experimental import pallas as pl
from jax.experimental.pallas import tpu as pltpu

F32 = jnp.float32
BF16 = jnp.bfloat16

D_MODEL = 2048
EPS = 1e-6
A_HEADS, A_DK, A_DV, A_CHUNK = 8, 128, 256, 64
B_HEADS, B_DH = 16, 128
Q_BLOCK = 128
C_GROUPS = ((128, 1), (512, 4), (2048, 16))
C_NG, C_HPG, C_DH = 3, 8, 128
N_BUCKETS, MAX_DISTANCE = 32, 2048
D_QK_HEADS, D_V_HEADS, D_DK, D_DV, D_CONV, D_CHUNK = 16, 32, 128, 128, 4, 64
D_CONV_CH = 2 * D_QK_HEADS * D_DK + D_V_HEADS * D_DV
D_FF, N_EXPERTS, TOP_K, D_FF_EXPERT = 5632, 8, 2, 7168
MOE_TM = 512
DMA_LAG = 32
DMA_UNROLL = 4
PAGE_SIZE = 128
LANE = 128
SB_TILE = 256
SB_HEADS_PER_STEP = 2

VMEM_LIMIT_BYTES = 56 * 1024 * 1024
WEIGHT_BLOCK_BYTES = 12 * 1024 * 1024
NEG_BIG = -1e30


def _cp(*sem):
    return pltpu.CompilerParams(dimension_semantics=sem, vmem_limit_bytes=VMEM_LIMIT_BYTES)


def _dot(a, b):
    return jnp.dot(a.astype(BF16), b.astype(BF16), preferred_element_type=F32)


def _dot_nt(a, b):
    return lax.dot_general(a.astype(BF16), b.astype(BF16), (((1,), (1,)), ((), ())),
                           preferred_element_type=F32)


def _dot_tn(a, b):
    return lax.dot_general(a.astype(BF16), b.astype(BF16), (((0,), (0,)), ((), ())),
                           preferred_element_type=F32)


def _split(a):
    hi = a.astype(BF16)
    return hi, (a - hi.astype(F32)).astype(BF16)


def _dot_hi(a, b):
    a_hi, a_lo = _split(a)
    b_hi, b_lo = _split(b)
    d = lambda x, y: jnp.dot(x, y, preferred_element_type=F32)
    return d(a_hi, b_hi) + (d(a_hi, b_lo) + d(a_lo, b_hi))


def _sigmoid(x):
    return 1.0 / (1.0 + jnp.exp(-x))


def _log_sigmoid(x):
    return jnp.minimum(x, 0.0) - jnp.log1p(jnp.exp(-jnp.abs(x)))


def _softplus(x):
    return jnp.maximum(x, 0.0) + jnp.log1p(jnp.exp(-jnp.abs(x)))


def _iota2(shape, axis):
    return lax.broadcasted_iota(jnp.int32, shape, axis)


def _row_to_col(row, eye):
    return jnp.sum(jnp.where(eye, row, 0.0), axis=1, keepdims=True)


def _pick_tn(K, N, col0=0):
    for tn in (2048, 1024, 512, 256, 128):
        if N % tn == 0 and col0 % tn == 0 and K * tn * 4 <= WEIGHT_BLOCK_BYTES:
            return tn
    raise ValueError((K, N, col0))


def _pick_tm(M, K):
    if M % 1024 == 0 and K <= 2048:
        return 1024
    return 512 if M % 512 == 0 else M


def _pick_rows(M, cap=1024):
    best = M
    for t in range(8, min(M, cap) + 1, 8):
        if M % t == 0:
            best = t
    return best


def _linear_kernel(x_ref, w_ref, o_ref, wbf_ref):
    @pl.when(pl.program_id(1) == 0)
    def _():
        wbf_ref[...] = w_ref[...].astype(BF16)

    o_ref[...] = jnp.dot(x_ref[...], wbf_ref[...], preferred_element_type=F32).astype(o_ref.dtype)


def _linear(x, w, *, sel=(), col0=0, ncols=None, out_dtype=F32):
    M, K = x.shape
    assert w.shape[-2] == K
    N = w.shape[-1] - col0 if ncols is None else ncols
    tn, tm = _pick_tn(K, N, col0), _pick_tm(M, K)
    off = col0 // tn
    w_spec = pl.BlockSpec((None,) * len(sel) + (K, tn), lambda j, m: tuple(sel) + (0, j + off))
    return pl.pallas_call(
        _linear_kernel, grid=(N // tn, M // tm),
        in_specs=[pl.BlockSpec((tm, K), lambda j, m: (m, 0)), w_spec],
        out_specs=pl.BlockSpec((tm, tn), lambda j, m: (m, j)),
        out_shape=jax.ShapeDtypeStruct((M, N), out_dtype),
        scratch_shapes=[pltpu.VMEM((K, tn), BF16)],
        compiler_params=_cp("arbitrary", "arbitrary"), name="linear",
    )(x, w)


def _gu_kernel(x_ref, wg_ref, wu_ref, o_ref, wg_bf, wu_bf):
    @pl.when(pl.program_id(1) == 0)
    def _():
        wg_bf[...] = wg_ref[...].astype(BF16)
        wu_bf[...] = wu_ref[...].astype(BF16)

    x = x_ref[...]
    g = jnp.dot(x, wg_bf[...], preferred_element_type=F32)
    u = jnp.dot(x, wu_bf[...], preferred_element_type=F32)
    o_ref[...] = (g * _sigmoid(g) * u).astype(o_ref.dtype)


def _swiglu_up(x, w, sel, F):
    M, K = x.shape
    tn, tm = 512, _pick_tm(M, K)
    lead = (None,) * len(sel)
    nb = F // tn
    return pl.pallas_call(
        _gu_kernel, grid=(nb, M // tm),
        in_specs=[pl.BlockSpec((tm, K), lambda j, m: (m, 0)),
                  pl.BlockSpec(lead + (K, tn), lambda j, m: tuple(sel) + (0, j)),
                  pl.BlockSpec(lead + (K, tn), lambda j, m: tuple(sel) + (0, j + nb))],
        out_specs=pl.BlockSpec((tm, tn), lambda j, m: (m, j)),
        out_shape=jax.ShapeDtypeStruct((M, F), BF16),
        scratch_shapes=[pltpu.VMEM((K, tn), BF16), pltpu.VMEM((K, tn), BF16)],
        compiler_params=_cp("arbitrary", "arbitrary"), name="swiglu_up",
    )(x, w, w)


def _router_kernel(x_ref, w_ref, o_ref):
    E = N_EXPERTS
    logits = jnp.dot(x_ref[...].astype(BF16), w_ref[...].astype(BF16), preferred_element_type=F32)
    lane = _iota2(logits.shape, 1)
    logits = jnp.where(lane < E, logits, -jnp.inf)
    m1 = jnp.max(logits, axis=1, keepdims=True)
    i1 = jnp.min(jnp.where(logits == m1, lane, LANE), axis=1, keepdims=True)
    rest = jnp.where(lane == i1, -jnp.inf, logits)
    m2 = jnp.max(rest, axis=1, keepdims=True)
    i2 = jnp.min(jnp.where(rest == m2, lane, LANE), axis=1, keepdims=True)
    e2 = jnp.exp(m2 - m1)
    g1 = 1.0 / (1.0 + e2)
    g2 = e2 / (1.0 + e2)
    o_ref[...] = (jnp.where(lane == i1, 1.0, 0.0) + jnp.where(lane == i2 + E, 1.0, 0.0)
                  + jnp.where(lane == 2 * E, g1, 0.0) + jnp.where(lane == 2 * E + 1, g2, 0.0))


def _router(x, w_pad):
    M, K = x.shape
    tm = _pick_rows(M)
    return pl.pallas_call(
        _router_kernel, grid=(M // tm,),
        in_specs=[pl.BlockSpec((tm, K), lambda m: (m, 0)), pl.BlockSpec((K, LANE), lambda m: (0, 0))],
        out_specs=pl.BlockSpec((tm, LANE), lambda m: (m, 0)),
        out_shape=jax.ShapeDtypeStruct((M, LANE), F32),
        compiler_params=_cp("arbitrary"), name="router",
    )(x, w_pad)


def _ada_kernel(c_ref, w_ref, b_ref, o_ref):
    c = c_ref[...]
    x = (c * _sigmoid(c)).astype(BF16)
    o_ref[0] = jnp.dot(x, w_ref[...].astype(BF16), preferred_element_type=F32) + b_ref[0]


def _ada(c_pad, w_ada, b_ada):
    R = c_pad.shape[0]
    depth, K, N = w_ada.shape
    tn = 1024
    return pl.pallas_call(
        _ada_kernel, grid=(depth, N // tn),
        in_specs=[pl.BlockSpec((R, K), lambda l, j: (0, 0)),
                  pl.BlockSpec((None, K, tn), lambda l, j: (l, 0, j)),
                  pl.BlockSpec((1, 1, tn), lambda l, j: (l, 0, j))],
        out_specs=pl.BlockSpec((1, R, tn), lambda l, j: (l, 0, j)),
        out_shape=jax.ShapeDtypeStruct((depth, R, N), F32),
        compiler_params=_cp("arbitrary", "arbitrary"), name="ada",
    )(c_pad, w_ada, b_ada.reshape(depth, 1, N))


def _norm_kernel(*refs, has_res, mod_rows, gate_row):
    refs = list(refs)
    x_ref = refs.pop(0)
    x = x_ref[0]
    if has_res:
        y_ref, gmod_ref = refs.pop(0), refs.pop(0)
        x = x + gmod_ref[0, gate_row:gate_row + 1, :] * y_ref[0]
    g_ref = refs.pop(0)
    nmod_ref = refs.pop(0) if mod_rows is not None else None
    if has_res and mod_rows is not None:
        xo_ref = refs.pop(0)
        xo_ref[0] = x
    h_ref = refs.pop(0)
    y = x * lax.rsqrt(jnp.mean(x * x, axis=-1, keepdims=True) + EPS) * g_ref[...]
    if mod_rows is not None:
        shift_row, scale_row = mod_rows
        y = y * (1.0 + nmod_ref[0, scale_row:scale_row + 1, :]) + nmod_ref[0, shift_row:shift_row + 1, :]
    h_ref[0] = y.astype(h_ref.dtype)


def _norm(x, g, *, y=None, gmod=None, gate_row=None, nmod=None, mod_rows=None, h_dtype=BF16):
    B, T, D = x.shape
    tt = min(T, 256)
    has_res = y is not None
    xs = pl.BlockSpec((1, tt, D), lambda b, t: (b, t, 0))
    ms = pl.BlockSpec((1, 6, D), lambda b, t: (b, 0, 0))
    args, specs = [x], [xs]
    if has_res:
        args += [y, gmod]
        specs += [xs, ms]
    args.append(g.reshape(1, D))
    specs.append(pl.BlockSpec((1, D), lambda b, t: (0, 0)))
    if mod_rows is not None:
        args.append(nmod)
        specs.append(ms)
    out_shape = [jax.ShapeDtypeStruct((B, T, D), h_dtype)]
    out_specs = [xs]
    if has_res and mod_rows is not None:
        out_shape.insert(0, jax.ShapeDtypeStruct((B, T, D), F32))
        out_specs.insert(0, xs)
    out = pl.pallas_call(
        functools.partial(_norm_kernel, has_res=has_res, mod_rows=mod_rows, gate_row=gate_row),
        grid=(B, T // tt), in_specs=specs, out_specs=out_specs, out_shape=out_shape,
        compiler_params=_cp("arbitrary", "arbitrary"), name="norm",
    )(*args)
    return out if len(out) > 1 else out[0]


def _mlstm_kernel(bif_ref, q_ref, k_ref, v_ref, og_ref, gi_ref, gf_ref, c0_ref, n0_ref, m0_ref, gn_ref,
                  hs_ref, c_out, n_out, m_out, C_s, n_s, m_s, *, L, nC):
    c = pl.program_id(1)
    H, DK, DV = A_HEADS, A_DK, A_DV
    heads = range(H)

    @pl.when(c == 0)
    def _():
        C_s[...] = c0_ref[0].reshape(H * DK, DV)
        n_s[...] = n0_ref[0]
        m_s[...] = m0_ref[0]

    row, col = _iota2((L, L), 0), _iota2((L, L), 1)
    eye, tril = row == col, col <= row
    qs = [q_ref[0, :, hd * DK:(hd + 1) * DK] * (DK ** -0.5) for hd in heads]
    ks = [k_ref[0, :, hd * DK:(hd + 1) * DK] for hd in heads]
    vs = [v_ref[0, :, hd * DV:(hd + 1) * DV] for hd in heads]
    Cs = [C_s[hd * DK:(hd + 1) * DK, :] for hd in heads]
    qks = [_dot_nt(qs[hd], ks[hd]) for hd in heads]
    qCs = [_dot(qs[hd], Cs[hd]) for hd in heads]
    m_ts, scs, Ds, decs, w_cols, m_news = [], [], [], [], [], []
    for hd in heads:
        li_row = gi_ref[hd, pl.ds(c, 1), :] + bif_ref[0, hd]
        lf_row = _log_sigmoid(gf_ref[hd, pl.ds(c, 1), :] + bif_ref[1, hd])
        li_col, lf_col = _row_to_col(li_row, eye), _row_to_col(lf_row, eye)
        b_col = jnp.sum(jnp.where(tril, lf_row, 0.0), axis=1, keepdims=True)
        b_row = jnp.sum(jnp.where(row <= col, lf_col, 0.0), axis=0, keepdims=True)
        b_last = jnp.sum(lf_row, axis=1, keepdims=True)
        m_prev = m_s[hd:hd + 1, :]
        dlog = jnp.where(tril, b_col - b_row + li_row, -jnp.inf)
        inter = b_col + m_prev
        m_t = jnp.maximum(inter, jnp.max(dlog, axis=1, keepdims=True))
        m_new = jnp.max(jnp.where(row[:, :1] == L - 1, m_t, -jnp.inf), axis=0, keepdims=True)
        m_ts.append(m_t)
        scs.append(jnp.exp(inter - m_t))
        Ds.append(jnp.exp(dlog - m_t))
        decs.append(jnp.exp(b_last + m_prev - m_new))
        w_cols.append(jnp.exp(b_last - b_col + li_col - m_new))
        m_news.append(m_new)
    s_qks = [qks[hd] * Ds[hd] for hd in heads]
    nums = [_dot(s_qks[hd], vs[hd]) + scs[hd] * qCs[hd] for hd in heads]
    kws = [ks[hd] * w_cols[hd] for hd in heads]
    C_new = jnp.concatenate([decs[hd] * Cs[hd] + _dot_tn(kws[hd], vs[hd]) for hd in heads], axis=0)
    row8 = _iota2((H, 1), 0)
    n_new, m_new, outs = jnp.zeros((H, DK), F32), jnp.zeros((H, 1), F32), []
    for hd in heads:
        n = n_s[hd:hd + 1, :]
        den = jnp.sum(s_qks[hd], axis=1, keepdims=True) + scs[hd] * jnp.sum(qs[hd] * n, axis=1, keepdims=True)
        hc = nums[hd] / jnp.maximum(jnp.abs(den), jnp.exp(-m_ts[hd]))
        hn = hc * lax.rsqrt(jnp.mean(hc * hc, axis=-1, keepdims=True) + EPS) * gn_ref[hd:hd + 1, :]
        outs.append((hn * _sigmoid(og_ref[0, :, hd * DV:(hd + 1) * DV])).astype(hs_ref.dtype))
        n_new = n_new + jnp.where(row8 == hd, decs[hd] * n + jnp.sum(kws[hd], axis=0, keepdims=True), 0.0)
        m_new = m_new + jnp.where(row8 == hd, m_news[hd], 0.0)
    C_s[...] = C_new
    n_s[...] = n_new
    m_s[...] = m_new
    hs_ref[0] = jnp.concatenate(outs, axis=1)

    @pl.when(c == nC - 1)
    def _():
        c_out[0] = C_new.reshape(H, DK, DV)
        n_out[0] = n_new
        m_out[0] = m_new


def _mlstm(h, C0, n0, m0, w):
    B, T, D = h.shape
    H, DK, DV = A_HEADS, A_DK, A_DV
    h2 = h.reshape(B * T, D)
    n_main = 2 * H * DK + 2 * H * DV
    proj = _linear(h2, w['mlstm_w_in'], ncols=n_main).reshape(B, T, n_main)
    w_gate = jnp.pad(w['mlstm_w_in'][:, n_main:], ((0, 0), (0, LANE - 2 * H)))
    gates = _linear(h2, w_gate)
    L = math.gcd(T, A_CHUNK)
    nC = T // L
    to_rows = lambda a: a.reshape(B, nC, L, H).transpose(0, 3, 1, 2).reshape(B * H, nC, L)
    gi, gf = to_rows(gates[:, :H]), to_rows(gates[:, H:2 * H])
    wk, wv = H * DK, H * DV
    cols = lambda wd, off: pl.BlockSpec((1, L, wd), lambda b, c: (b, c, off))
    gs = pl.BlockSpec((H, nC, L), lambda b, c: (b, 0, 0))
    cs = pl.BlockSpec((1, H, DK, DV), lambda b, c: (b, 0, 0, 0))
    ns = pl.BlockSpec((1, H, DK), lambda b, c: (b, 0, 0))
    ms = pl.BlockSpec((1, H, 1), lambda b, c: (b, 0, 0))
    hs, C, n, m = pl.pallas_call(
        functools.partial(_mlstm_kernel, L=L, nC=nC), grid=(B, nC),
        in_specs=[pl.BlockSpec(memory_space=pltpu.SMEM), cols(wk, 0), cols(wk, 1), cols(wv, 2 * wk // wv),
                  cols(wv, 2 * wk // wv + 1), gs, gs, cs, ns, ms, pl.BlockSpec((H, DV), lambda b, c: (0, 0))],
        out_specs=[cols(wv, 0), cs, ns, ms],
        out_shape=[jax.ShapeDtypeStruct((B, T, H * DV), BF16), jax.ShapeDtypeStruct((B, H, DK, DV), F32),
                   jax.ShapeDtypeStruct((B, H, DK), F32), jax.ShapeDtypeStruct((B, H, 1), F32)],
        scratch_shapes=[pltpu.VMEM((H * DK, DV), F32), pltpu.VMEM((H, DK), F32), pltpu.VMEM((H, 1), F32)],
        compiler_params=_cp("arbitrary", "arbitrary"), name="mlstm",
    )(w['mlstm_b_if'], proj, proj, proj, proj, gi, gf, C0.astype(F32), n0.astype(F32),
      m0.astype(F32).reshape(B, H, 1), w['mlstm_norm_g'].astype(F32))
    y = _linear(hs.reshape(B * T, H * DV), w['mlstm_w_out'])
    return y.reshape(B, T, D), (C, n, m.reshape(B, H))


def _sb_weights(zs, valid, laters, upper):
    lss = [_log_sigmoid(z) for z in zs]
    l1s = [ls - z for ls, z in zip(lss, zs)]
    if valid is not None:
        l1s = [jnp.where(valid, l1, 0.0) for l1 in l1s]
    his = [l1.astype(BF16) for l1 in l1s]
    los = [(l1 - hi.astype(F32)).astype(BF16) for l1, hi in zip(l1s, his)]
    afters = [jnp.dot(hi, upper, preferred_element_type=F32) + jnp.dot(lo, upper, preferred_element_type=F32)
              for hi, lo in zip(his, los)]
    as_ = [jnp.exp(ls + after + later) for ls, after, later in zip(lss, afters, laters)]
    if valid is not None:
        as_ = [jnp.where(valid, a, 0.0) for a in as_]
    return as_, [later + jnp.sum(l1, axis=1, keepdims=True) for later, l1 in zip(laters, l1s)]


def _sb_kernel(bias_ref, q_ref, k_ref, v_ref, o_ref, *, TQ, HG):
    hg, i = pl.program_id(1), pl.program_id(2)
    DH = B_DH
    heads = range(HG)
    cols = lambda hd: slice(hd * DH, (hd + 1) * DH)
    qs = [q_ref[0, :, cols(hd)] * (DH ** -0.5) for hd in heads]
    biases = [bias_ref[hg * HG + hd] for hd in heads]
    row, col = _iota2((TQ, TQ), 0), _iota2((TQ, TQ), 1)
    upper = jnp.where(row > col, 1.0, 0.0).astype(BF16)

    def tile(kk, carry, valid):
        outs, laters = carry
        keys = pl.ds(pl.multiple_of((i - kk) * TQ, TQ), TQ)
        zs = [_dot_nt(qs[hd], k_ref[0, keys, cols(hd)]) + biases[hd] for hd in heads]
        as_, laters = _sb_weights(zs, valid, laters, upper)
        return [outs[hd] + _dot(as_[hd], v_ref[0, keys, cols(hd)]) for hd in heads], laters

    init = ([jnp.zeros((TQ, DH), F32) for _ in heads], [jnp.zeros((TQ, 1), F32) for _ in heads])
    carry = tile(0, init, col < row)
    outs, _ = lax.fori_loop(1, i + 1, lambda kk, c: tile(kk, c, None), carry)
    o_ref[0] = jnp.concatenate(outs, axis=1).astype(o_ref.dtype)


def _sb_prompt(proj, bias):
    B, T, _ = proj.shape
    H, HG = B_HEADS, SB_HEADS_PER_STEP
    TQ = math.gcd(T, SB_TILE)
    wd = HG * B_DH
    kvs = lambda off: pl.BlockSpec((1, T, wd), lambda b, hg, i: (b, 0, off + hg))
    qo = pl.BlockSpec((1, TQ, wd), lambda b, hg, i: (b, i, hg))
    return pl.pallas_call(
        functools.partial(_sb_kernel, TQ=TQ, HG=HG), grid=(B, H // HG, T // TQ),
        in_specs=[pl.BlockSpec(memory_space=pltpu.SMEM), qo, kvs(H // HG), kvs(2 * H // HG)],
        out_specs=qo,
        out_shape=jax.ShapeDtypeStruct((B, T, H * B_DH), BF16),
        compiler_params=_cp("arbitrary", "arbitrary", "arbitrary"), name="sb_prompt",
    )(bias, proj, proj, proj)


def _sb_dec_kernel(pt_ref, q_ref, bias_ref, kvn_ref, cache_hbm, o_ref, kv_buf, sem, acc_s, later_s,
                   *, n_seq, n_pages, TN):
    b, p = pl.program_id(0), pl.program_id(1)
    step = b * n_pages + p
    slot = lax.rem(step, 2)
    H = B_HEADS
    R = H * TN

    def page_copies(bb, pp, sl):
        page = pt_ref[bb, n_pages - 1 - pp]
        return [pltpu.make_async_copy(cache_hbm.at[page, :, kv, hd, :], kv_buf.at[sl, kv * H + hd], sem.at[sl])
                for kv in range(2) for hd in range(H)]

    @pl.when(step == 0)
    def _():
        for cp in page_copies(0, 0, 0):
            cp.start()

    @pl.when(step + 1 < n_seq * n_pages)
    def _():
        wrap = p + 1 == n_pages
        for cp in page_copies(jnp.where(wrap, b + 1, b), jnp.where(wrap, 0, p + 1), 1 - slot):
            cp.start()

    bias = bias_ref[...]
    row, col = _iota2((PAGE_SIZE, PAGE_SIZE), 0), _iota2((PAGE_SIZE, PAGE_SIZE), 1)
    upper = jnp.where(row > col, 1.0, 0.0).astype(BF16)

    def segment(rows_of, valid, later):
        z = jnp.concatenate([_dot_nt(q_ref[0, hd], rows_of(hd)) for hd in range(H)], axis=0) + bias
        (a,), (later,) = _sb_weights([z], valid, [later], upper)
        out = jnp.concatenate([_dot(a[hd * TN:(hd + 1) * TN], rows_of(H + hd)) for hd in range(H)], axis=0)
        return out, later

    @pl.when(p == 0)
    def _():
        rq, ck = _iota2((R, PAGE_SIZE), 0), _iota2((R, PAGE_SIZE), 1)
        out, later = segment(lambda i: kvn_ref[0, i], ck < lax.rem(rq, TN), jnp.zeros((R, 1), F32))
        acc_s[...] = out
        later_s[...] = later

    for cp in page_copies(b, p, slot):
        cp.wait()
    out, later = segment(lambda i: kv_buf[slot, i], None, later_s[...])
    acc_s[...] += out
    later_s[...] = later

    @pl.when(p == n_pages - 1)
    def _():
        o_ref[0] = acc_s[...]


def _sb_decode(proj, cache, page_table, bias):
    B, TN, _ = proj.shape
    H, DH = B_HEADS, B_DH
    HD = H * DH
    R = H * TN
    n_pages = page_table.shape[1]
    q = proj[:, :, :HD].reshape(B, TN, H, DH).transpose(0, 2, 1, 3) * (DH ** -0.5)
    kvn = proj[:, :, HD:].reshape(B, TN, 2 * H, DH).transpose(0, 2, 1, 3)
    kvn = jnp.pad(kvn, ((0, 0), (0, 0), (0, PAGE_SIZE - TN), (0, 0)))
    bias_col = jnp.repeat(bias.astype(F32), TN).reshape(R, 1)
    grid_spec = pltpu.PrefetchScalarGridSpec(
        num_scalar_prefetch=1, grid=(B, n_pages),
        in_specs=[pl.BlockSpec((1, H, TN, DH), lambda b, p, pt: (b, 0, 0, 0)),
                  pl.BlockSpec((R, 1), lambda b, p, pt: (0, 0)),
                  pl.BlockSpec((1, 2 * H, PAGE_SIZE, DH), lambda b, p, pt: (b, 0, 0, 0)),
                  pl.BlockSpec(memory_space=pl.ANY)],
        out_specs=pl.BlockSpec((1, R, DH), lambda b, p, pt: (b, 0, 0)),
        scratch_shapes=[pltpu.VMEM((2, 2 * H, PAGE_SIZE, DH), F32), pltpu.SemaphoreType.DMA((2,)),
                        pltpu.VMEM((R, DH), F32), pltpu.VMEM((R, 1), F32)])
    o = pl.pallas_call(
        functools.partial(_sb_dec_kernel, n_seq=B, n_pages=n_pages, TN=TN), grid_spec=grid_spec,
        out_shape=jax.ShapeDtypeStruct((B, R, DH), F32),
        compiler_params=_cp("arbitrary", "arbitrary"), name="sb_decode",
    )(page_table, q, bias_col, kvn, cache)
    return o.reshape(B, H, TN, DH).transpose(0, 2, 1, 3).reshape(B, TN, HD).astype(BF16)


def _sb_mixer(h, past, w):
    B, T, D = h.shape
    proj = _linear(h.reshape(B * T, D), w['sb_w_qkv']).reshape(B, T, 3 * B_HEADS * B_DH)
    if past is None:
        o = _sb_prompt(proj, w['sb_bias'])
    else:
        o = _sb_decode(proj, past[0], past[1], w['sb_bias'])
    y = _linear(o.reshape(B * T, B_HEADS * B_DH), w['sb_w_out'])
    kv_new = proj[:, :, B_HEADS * B_DH:].reshape(B, T, 2, B_HEADS, B_DH)
    return y.reshape(B, T, D), kv_new


def _t5_bucket(dist):
    max_exact = N_BUCKETS // 2
    large = max_exact + (jnp.log(jnp.maximum(dist, 1).astype(F32) / max_exact)
                         / math.log(MAX_DISTANCE / max_exact) * (N_BUCKETS - max_exact)).astype(jnp.int32)
    return jnp.where(dist < max_exact, dist, jnp.minimum(large, N_BUCKETS - 1))


def _tap_bias(rel_bias, g, taps, valid):
    win, dil = C_GROUPS[g]
    J = win // dil + 1
    valid = jnp.logical_and(valid, jnp.logical_and(taps >= 0, taps < J))
    tab = rel_bias[_t5_bucket(dil * jnp.arange(J))][:, g * C_HPG:(g + 1) * C_HPG].astype(F32)
    hit = taps[None, ..., None] == jnp.arange(J)
    vals = jnp.sum(jnp.where(hit, tab.T.reshape((C_HPG,) + (1,) * taps.ndim + (J,)), 0.0), axis=-1)
    return jnp.where(valid[None], vals, -jnp.inf)


def _band_kernel(q_ref, kp_ref, kc_ref, vp_ref, vc_ref, bias_ref, o_ref, lse_ref, *, TQ):
    i = pl.program_id(2)
    H, DH = C_HPG, C_DH
    heads = range(H)
    head = lambda ref, hd: ref[0, :, hd * DH:(hd + 1) * DH]
    qs = [head(q_ref, hd) * (DH ** -0.5) for hd in heads]
    s_cs = [_dot_nt(qs[hd], head(kc_ref, hd)) + bias_ref[hd, :, TQ:] for hd in heads]
    s_ps = [jnp.where(i > 0, _dot_nt(qs[hd], head(kp_ref, hd)) + bias_ref[hd, :, :TQ], -jnp.inf) for hd in heads]
    mxs = [jnp.maximum(jnp.max(s_cs[hd], axis=1, keepdims=True), jnp.max(s_ps[hd], axis=1, keepdims=True))
           for hd in heads]
    p_cs = [jnp.exp(s_cs[hd] - mxs[hd]) for hd in heads]
    p_ps = [jnp.exp(s_ps[hd] - mxs[hd]) for hd in heads]
    ls = [jnp.sum(p_cs[hd], axis=1, keepdims=True) + jnp.sum(p_ps[hd], axis=1, keepdims=True) for hd in heads]
    outs = [(_dot(p_cs[hd], head(vc_ref, hd)) + _dot(p_ps[hd], head(vp_ref, hd))) / ls[hd] for hd in heads]
    o_ref[0] = jnp.concatenate(outs, axis=1)
    lane = _iota2((TQ, H), 1)
    lse = jnp.zeros((TQ, H), F32)
    for hd in heads:
        lse = lse + jnp.where(lane == hd, mxs[hd] + jnp.log(ls[hd]), 0.0)
    lse_ref[0, 0] = lse


def _dw_prompt_group(proj, rel_bias, g):
    B, T, W3 = proj.shape
    win, dil = C_GROUPS[g]
    H, DH = C_HPG, C_DH
    TQ = win // dil
    Ts = T // dil
    assert Ts % TQ == 0
    HD = H * DH
    nb = 3
    cols = [proj[:, :, (which * C_NG + g) * HD:(which * C_NG + g + 1) * HD] for which in range(nb)]
    pv = jnp.concatenate(cols, axis=2).reshape(B, Ts, dil * nb * HD)
    t_loc, s_loc = jnp.arange(TQ)[:, None], jnp.arange(2 * TQ)[None, :] - TQ
    bias = _tap_bias(rel_bias, g, t_loc - s_loc, jnp.ones((TQ, 2 * TQ), bool))
    blk = lambda which, prev: pl.BlockSpec(
        (1, TQ, HD), lambda b, r, i: (b, jnp.maximum(i - 1, 0) if prev else i, r * nb + which))
    o, lse = pl.pallas_call(
        functools.partial(_band_kernel, TQ=TQ), grid=(B, dil, Ts // TQ),
        in_specs=[blk(0, False), blk(1, True), blk(1, False), blk(2, True), blk(2, False),
                  pl.BlockSpec((H, TQ, 2 * TQ), lambda b, r, i: (0, 0, 0))],
        out_specs=[pl.BlockSpec((1, TQ, HD), lambda b, r, i: (b, i, r)),
                   pl.BlockSpec((1, 1, TQ, H), lambda b, r, i: (b, r, i, 0))],
        out_shape=[jax.ShapeDtypeStruct((B, Ts, dil * HD), F32),
                   jax.ShapeDtypeStruct((B, dil, Ts, H), F32)],
        compiler_params=_cp("arbitrary", "arbitrary", "arbitrary"), name="dw_band",
    )(pv, pv, pv, pv, pv, bias)
    lse = lse.transpose(0, 2, 1, 3).reshape(B, T, H)
    return o.reshape(B, T, H * DH), lse


def _dw_dec_kernel(q_ref, bmn_ref, bm_ref, kvn_ref, buf_ref, o_ref, lse_ref, m_s, l_s, acc_s, *, n_tiles, TN):
    wi = pl.program_id(1)
    H = C_HPG
    R = H * TN

    def segment(src_ref, bm, m_old, l_old, acc_old):
        n_keys = bm.shape[1]
        head_rows = lambda first: src_ref[pl.ds(0, 1), pl.ds(first, n_keys, stride=2 * H), :][0]
        s = jnp.concatenate([_dot_nt(q_ref[0, hd], head_rows(hd)) for hd in range(H)], axis=0) + bm
        m_new = jnp.maximum(m_old, jnp.max(s, axis=1, keepdims=True))
        alpha = jnp.exp(m_old - m_new)
        p = jnp.exp(s - m_new)
        pv = jnp.concatenate([_dot(p[hd * TN:(hd + 1) * TN], head_rows(H + hd)) for hd in range(H)], axis=0)
        m_s[...] = m_new
        l_s[...] = alpha * l_old + jnp.sum(p, axis=1, keepdims=True)
        acc_s[...] = alpha * acc_old + pv

    @pl.when(wi == 0)
    def _():
        segment(kvn_ref, bmn_ref[...], jnp.full((R, 1), NEG_BIG, F32), jnp.zeros((R, 1), F32),
                jnp.zeros((R, C_DH), F32))

    segment(buf_ref, bm_ref[...], m_s[...], l_s[...], acc_s[...])

    @pl.when(wi == n_tiles - 1)
    def _():
        l = l_s[...]
        lse_ref[0] = m_s[...] + jnp.log(l)
        o_ref[0] = acc_s[...] / l


def _dw_decode_group(proj, buf, rel_bias, g):
    B, TN, _ = proj.shape
    win, dil = C_GROUPS[g]
    H, DH = C_HPG, C_DH
    HD = H * DH
    R = H * TN
    W = buf.shape[1]
    TW = min(W, 512)
    p6 = proj.reshape(B, TN, 3, C_NG, H, DH)
    q = p6[:, :, 0, g].transpose(0, 2, 1, 3) * (DH ** -0.5)
    kvn = jnp.pad(p6[:, :, 1:, g].reshape(B, TN * 2 * H, DH), ((0, 0), (0, (LANE - TN) * 2 * H), (0, 0)))
    t = jnp.arange(TN)[:, None]
    dist_buf = W + t - jnp.arange(W)[None, :]
    dist_new = t - jnp.arange(LANE)[None, :]
    bias_of = lambda dist, ok: _tap_bias(rel_bias, g, dist // dil, jnp.logical_and(ok, dist % dil == 0))
    bm = bias_of(dist_buf, jnp.ones_like(dist_buf, bool)).reshape(R, W)
    bmn = bias_of(dist_new, jnp.arange(LANE)[None, :] < TN).reshape(R, LANE)
    o, lse = pl.pallas_call(
        functools.partial(_dw_dec_kernel, n_tiles=W // TW, TN=TN), grid=(B, W // TW),
        in_specs=[pl.BlockSpec((1, H, TN, DH), lambda b, wi: (b, 0, 0, 0)),
                  pl.BlockSpec((R, LANE), lambda b, wi: (0, 0)),
                  pl.BlockSpec((R, TW), lambda b, wi: (0, wi)),
                  pl.BlockSpec((1, LANE * 2 * H, DH), lambda b, wi: (b, 0, 0)),
                  pl.BlockSpec((1, TW * 2 * H, DH), lambda b, wi: (b, wi, 0))],
        out_specs=[pl.BlockSpec((1, R, DH), lambda b, wi: (b, 0, 0)),
                   pl.BlockSpec((1, R, 1), lambda b, wi: (b, 0, 0))],
        out_shape=[jax.ShapeDtypeStruct((B, R, DH), F32), jax.ShapeDtypeStruct((B, R, 1), F32)],
        scratch_shapes=[pltpu.VMEM((R, 1), F32), pltpu.VMEM((R, 1), F32), pltpu.VMEM((R, DH), F32)],
        compiler_params=_cp("arbitrary", "arbitrary"), name="dw_decode",
    )(q, bmn, bm, kvn, buf.reshape(B, W * 2 * H, DH))
    o = o.reshape(B, H, TN, DH).transpose(0, 2, 1, 3).reshape(B, TN, HD)
    lse = lse.reshape(B, H, TN).transpose(0, 2, 1)
    return o, lse


def _dw_combine_kernel(o0, o1, o2, l0, l1, l2, out_ref):
    ls = [l0[0], l1[0], l2[0]]
    mx = jnp.maximum(jnp.maximum(ls[0], ls[1]), ls[2])
    es = [jnp.exp(l - mx) for l in ls]
    tot = es[0] + es[1] + es[2]
    ws = [e / tot for e in es]
    for hd in range(C_HPG):
        cols = slice(hd * C_DH, (hd + 1) * C_DH)
        acc = ws[0][:, hd:hd + 1] * o0[0, :, cols]
        acc = acc + ws[1][:, hd:hd + 1] * o1[0, :, cols]
        acc = acc + ws[2][:, hd:hd + 1] * o2[0, :, cols]
        out_ref[0, :, cols] = acc.astype(out_ref.dtype)


def _dw_combine(outs, lses):
    B, T, HD = outs[0].shape
    tt = min(T, 256)
    os_ = pl.BlockSpec((1, tt, HD), lambda b, t: (b, t, 0))
    ls_ = pl.BlockSpec((1, tt, C_HPG), lambda b, t: (b, t, 0))
    return pl.pallas_call(
        _dw_combine_kernel, grid=(B, T // tt), in_specs=[os_] * 3 + [ls_] * 3, out_specs=os_,
        out_shape=jax.ShapeDtypeStruct((B, T, HD), BF16),
        compiler_params=_cp("arbitrary", "arbitrary"), name="dw_combine",
    )(*outs, *lses)


def _dw_mixer(h, bufs, w):
    B, T, D = h.shape
    W3 = 3 * C_NG * C_HPG * C_DH
    proj = _linear(h.reshape(B * T, D), w['dw_w_qkv']).reshape(B, T, W3)
    HD = C_HPG * C_DH

    def group_kv(g, first_row):
        k, v = (proj[:, first_row:, (which * C_NG + g) * HD:(which * C_NG + g + 1) * HD] for which in (1, 2))
        return jnp.stack([k, v], axis=2).reshape(B, T - first_row, 2, C_HPG, C_DH)

    outs, lses, new_bufs = [], [], []
    for g, (win, dil) in enumerate(C_GROUPS):
        if bufs is None:
            o, lse = _dw_prompt_group(proj, w['rel_bias'], g)
            new_bufs.append(group_kv(g, T - min(win, T)))
        else:
            o, lse = _dw_decode_group(proj, bufs[g], w['rel_bias'], g)
            new_bufs.append(jnp.concatenate([bufs[g].astype(F32), group_kv(g, 0)], axis=1)[:, T:])
        outs.append(o)
        lses.append(lse)
    o = _dw_combine(outs, lses)
    y = _linear(o.reshape(B * T, C_HPG * C_DH), w['dw_w_out'])
    return y.reshape(B, T, D), tuple(new_bufs)


def _conv_kernel(x_ref, halo_ref, w_ref, o_ref):
    x = x_ref[0]
    halo = halo_ref[0, 0]
    tt = x.shape[0]
    head = x[:8]
    r8 = _iota2(head.shape, 0)
    acc = x * w_ref[D_CONV - 1:D_CONV, :]
    for s in range(1, D_CONV):
        top = jnp.where(r8 < s, pltpu.roll(halo, s, axis=0), pltpu.roll(head, s, axis=0))
        if tt > 8:
            shifted = jnp.concatenate([top, pltpu.roll(x, s, axis=0)[8:]], axis=0)
        else:
            shifted = top
        acc = acc + shifted * w_ref[D_CONV - 1 - s:D_CONV - s, :]
    o_ref[0] = acc * _sigmoid(acc)


def _gdn_conv(proj, conv_buf, conv_w):
    B, T, _ = proj.shape
    C = D_CONV_CH
    tt, tc = min(T, 256), 1024
    nT = T // tt
    first = jnp.pad(conv_buf.astype(F32), ((0, 0), (8 - (D_CONV - 1), 0), (0, 0)))[:, None]
    if nT > 1:
        tails = proj[:, :, :C].reshape(B, nT, tt, C)[:, :-1, tt - 8:]
        halo = jnp.concatenate([first, tails], axis=1)
    else:
        halo = first
    return pl.pallas_call(
        _conv_kernel, grid=(B, nT, C // tc),
        in_specs=[pl.BlockSpec((1, tt, tc), lambda b, t, c: (b, t, c)),
                  pl.BlockSpec((1, 1, 8, tc), lambda b, t, c: (b, t, 0, c)),
                  pl.BlockSpec((D_CONV, tc), lambda b, t, c: (0, c))],
        out_specs=pl.BlockSpec((1, tt, tc), lambda b, t, c: (b, t, c)),
        out_shape=jax.ShapeDtypeStruct((B, T, C), F32),
        compiler_params=_cp("arbitrary", "arbitrary", "arbitrary"), name="gdn_conv",
    )(proj, halo, conv_w.astype(F32))


def _gdn_kernel(par_ref, q_ref, k_ref, v_ref, z_ref, braw_ref, araw_ref, s0_ref, gn_ref, o_ref, s_out, S_s,
                *, L, nC, HG, rep):
    hg, c = pl.program_id(1), pl.program_id(2)
    DK, DV = D_DK, D_DV

    @pl.when(c == 0)
    def _():
        S_s[...] = s0_ref[0].reshape(HG * DK, DV)

    row, col = _iota2((L, L), 0), _iota2((L, L), 1)
    eye, tril = row == col, col <= row
    ident = jnp.where(eye, 1.0, 0.0)
    heads = range(HG)
    qs, ks, kks, qks = [], [], [], []
    for jq in range(HG // rep):
        q, k = q_ref[0, :, jq * DK:(jq + 1) * DK], k_ref[0, :, jq * DK:(jq + 1) * DK]
        q = q * lax.rsqrt(jnp.sum(q * q, axis=-1, keepdims=True) + EPS) * (DK ** -0.5)
        k = k * lax.rsqrt(jnp.sum(k * k, axis=-1, keepdims=True) + EPS)
        qs += [q] * rep
        ks += [k] * rep
        kks += [_dot_nt(k, k)] * rep
        qks += [_dot_nt(q, k)] * rep
    betas, Gs, GLs, decays, Ns = [], [], [], [], []
    for j in heads:
        hd = hg * HG + j
        beta_row = _sigmoid(braw_ref[j, pl.ds(c, 1), :])
        g_row = -jnp.exp(par_ref[0, hd]) * _softplus(araw_ref[j, pl.ds(c, 1), :] + par_ref[1, hd])
        beta_col, g_col = _row_to_col(beta_row, eye), _row_to_col(g_row, eye)
        G_col = jnp.sum(jnp.where(tril, g_row, 0.0), axis=1, keepdims=True)
        G_row = jnp.sum(jnp.where(row <= col, g_col, 0.0), axis=0, keepdims=True)
        decay = jnp.exp(jnp.where(tril, G_col - G_row, -jnp.inf))
        betas.append(beta_col)
        Gs.append(G_col)
        GLs.append(jnp.sum(g_row, axis=1, keepdims=True))
        decays.append(decay)
        Ns.append(jnp.where(col < row, -(beta_col * kks[j]) * decay, 0.0))
    invs = [ident + N for N in Ns]
    for _ in range(int(math.log2(L)) - 1):
        Ns = [_dot_hi(N, N) for N in Ns]
        invs = [inv + _dot_hi(inv, N) for inv, N in zip(invs, Ns)]
    eGs = [jnp.exp(G) for G in Gs]
    Us = [_dot_hi(invs[j], v_ref[0, :, j * DV:(j + 1) * DV] * betas[j]) for j in heads]
    Ws = [_dot_hi(invs[j], ks[j] * (betas[j] * eGs[j])) for j in heads]
    Ss = [S_s[j * DK:(j + 1) * DK, :] for j in heads]
    v_news = [Us[j] - _dot(Ws[j], Ss[j]) for j in heads]
    os_ = [_dot(qs[j] * eGs[j], Ss[j]) + _dot(qks[j] * decays[j], v_news[j]) for j in heads]
    states = [jnp.exp(GLs[j]) * Ss[j] + _dot_tn(ks[j] * jnp.exp(GLs[j] - Gs[j]), v_news[j]) for j in heads]
    outs = []
    for j in heads:
        o, z = os_[j], z_ref[0, :, j * DV:(j + 1) * DV]
        on = o * lax.rsqrt(jnp.mean(o * o, axis=-1, keepdims=True) + EPS) * gn_ref[...]
        outs.append((on * (z * _sigmoid(z))).astype(o_ref.dtype))
    S_new = jnp.concatenate(states, axis=0)
    S_s[...] = S_new
    o_ref[0] = jnp.concatenate(outs, axis=1)

    @pl.when(c == nC - 1)
    def _():
        s_out[0] = S_new.reshape(HG, DK, DV)


def _gdn_mixer(h, conv_buf, S0, w):
    B, T, D = h.shape
    HQ, HV, DK, DV = D_QK_HEADS, D_V_HEADS, D_DK, D_DV
    C = D_CONV_CH
    n_v = HV * DV
    n_main = C + n_v
    h2 = h.reshape(B * T, D)
    proj = _linear(h2, w['gdn_w_in'], ncols=n_main).reshape(B, T, n_main)
    w_ba = jnp.pad(w['gdn_w_in'][:, n_main:], ((0, 0), (0, LANE - 2 * HV)))
    ba = _linear(h2, w_ba)
    conv = _gdn_conv(proj, conv_buf, w['gdn_conv_w'])
    new_buf = jnp.concatenate([conv_buf.astype(F32), proj[:, :, :C]], axis=1)[:, T:] if T < D_CONV - 1 \
        else proj[:, T - (D_CONV - 1):, :C]
    L = math.gcd(T, D_CHUNK)
    nC = T // L
    to_rows = lambda a: a.reshape(B, nC, L, HV).transpose(0, 3, 1, 2).reshape(B * HV, nC, L)
    braw, araw = to_rows(ba[:, :HV]), to_rows(ba[:, HV:2 * HV])
    par = jnp.stack([w['gdn_A_log'], w['gdn_dt_bias']]).astype(F32)
    rep = HV // HQ
    HG = 8
    wq, wv = HG // rep * DK, HG * DV
    blk = lambda wd, off: pl.BlockSpec((1, L, wd), lambda b, hg, c: (b, c, off + hg))
    gs = pl.BlockSpec((HG, nC, L), lambda b, hg, c: (b * (HV // HG) + hg, 0, 0))
    ss = pl.BlockSpec((1, HG, DK, DV), lambda b, hg, c: (b, hg, 0, 0))
    o, S = pl.pallas_call(
        functools.partial(_gdn_kernel, L=L, nC=nC, HG=HG, rep=rep), grid=(B, HV // HG, nC),
        in_specs=[pl.BlockSpec(memory_space=pltpu.SMEM),
                  blk(wq, 0), blk(wq, HQ * DK // wq), blk(wv, 2 * HQ * DK // wv), blk(wv, C // wv), gs, gs, ss,
                  pl.BlockSpec((1, DV), lambda b, hg, c: (0, 0))],
        out_specs=[blk(wv, 0), ss],
        out_shape=[jax.ShapeDtypeStruct((B, T, n_v), BF16), jax.ShapeDtypeStruct((B, HV, DK, DV), F32)],
        scratch_shapes=[pltpu.VMEM((HG * DK, DV), F32)],
        compiler_params=_cp("arbitrary", "arbitrary", "arbitrary"), name="gdn",
    )(par, conv, conv, conv, proj, braw, araw, S0.astype(F32), w['gdn_norm_g'].reshape(1, DV).astype(F32))
    y = _linear(o.reshape(B * T, n_v), w['gdn_w_out'])
    return y.reshape(B, T, D), (new_buf, S)


def _ffn(h, w, i):
    B, T, D = h.shape
    act = _swiglu_up(h.reshape(B * T, D), w['ffn_w_gu'], (i,), D_FF)
    return _linear(act, w['ffn_w_down'], sel=(i,)).reshape(B, T, D)


def _moe_plan(route, n_tiles):
    E, TM = N_EXPERTS, MOE_TM
    sel1, sel2 = route[:, :E], route[:, E:2 * E]
    cnt1 = jnp.sum(sel1, axis=0)
    cnt = cnt1 + jnp.sum(sel2, axis=0)
    pcnt = jnp.ceil(cnt / TM) * TM
    pend = jnp.cumsum(pcnt)
    pstart = pend - pcnt
    rank1 = jnp.cumsum(sel1, axis=0) - sel1
    rank2 = cnt1[None] + jnp.cumsum(sel2, axis=0) - sel2
    dest1 = jnp.sum(sel1 * (pstart[None] + rank1), axis=1)
    dest2 = jnp.sum(sel2 * (pstart[None] + rank2), axis=1)
    dest = jnp.concatenate([dest1, dest2]).astype(jnp.int32)
    n_used = (pend[-1] / TM).astype(jnp.int32)
    first_row = jnp.minimum(jnp.arange(n_tiles), n_used - 1).astype(F32) * TM
    tile_expert = jnp.minimum(jnp.sum((first_row[:, None] >= pend[None, :]).astype(jnp.int32), axis=1), E - 1)
    M = route.shape[0]
    token_of = (jnp.argsort(dest) % M).astype(jnp.int32)
    of_tile = (tile_expert[:, None] == jnp.arange(E)[None, :]).astype(F32)
    per_row = lambda v: jnp.repeat(jnp.sum(of_tile * v[None], axis=1), TM)
    rank = jnp.arange(n_tiles * TM).astype(F32) - per_row(pstart)
    packed = per_row(jnp.cumsum(cnt) - cnt) + rank
    src = jnp.where(jnp.logical_and(rank >= 0, rank < per_row(cnt)),
                    token_of[jnp.clip(packed, 0, TOP_K * M - 1).astype(jnp.int32)], 0)
    return dest, src, tile_expert, n_used.reshape(1)


def _row_copy(src_hbm, src_row, dst, dst_row, sem):
    return pltpu.make_async_copy(src_hbm.at[pl.ds(src_row, 1)], dst.at[pl.ds(dst_row, 1)], sem)


def _lagged_copies(n, copies_of, lag):
    def body(t, carry):
        for cp in copies_of(t):
            cp.start()

        @pl.when(t >= lag)
        def _():
            for cp in copies_of(t - lag):
                cp.wait()
        return carry

    def drain(t, carry):
        for cp in copies_of(t):
            cp.wait()
        return carry

    lax.fori_loop(0, n, body, 0, unroll=DMA_UNROLL if n % DMA_UNROLL == 0 else 1)
    lax.fori_loop(max(n - lag, 0), n, drain, 0)


def _dispatch_kernel(src_ref, h_hbm, o_ref, buf, sem, *, TR):
    base = pl.program_id(0) * TR
    _lagged_copies(TR, lambda r: [_row_copy(h_hbm, src_ref[base + r], buf, r, sem)], DMA_LAG)
    o_ref[...] = buf[...].astype(o_ref.dtype)


def _dispatch(h, src):
    D = h.shape[1]
    P = src.shape[0]
    TR = MOE_TM
    grid_spec = pltpu.PrefetchScalarGridSpec(
        num_scalar_prefetch=1, grid=(P // TR,), in_specs=[pl.BlockSpec(memory_space=pl.ANY)],
        out_specs=pl.BlockSpec((TR, D), lambda t, src: (t, 0)),
        scratch_shapes=[pltpu.VMEM((TR, D), F32), pltpu.SemaphoreType.DMA(())])
    return pl.pallas_call(
        functools.partial(_dispatch_kernel, TR=TR), grid_spec=grid_spec,
        out_shape=jax.ShapeDtypeStruct((P, D), BF16),
        compiler_params=_cp("arbitrary"), name="moe_dispatch",
    )(src, h)


def _tile_is_new(te_ref, m):
    return jnp.logical_or(m == 0, te_ref[m] != te_ref[jnp.maximum(m - 1, 0)])


def _moe_up_kernel(te_ref, nu_ref, x_ref, wg_ref, wu_ref, o_ref, wg_bf, wu_bf):
    m = pl.program_id(1)

    @pl.when(_tile_is_new(te_ref, m))
    def _():
        wg_bf[...] = wg_ref[...].astype(BF16)
        wu_bf[...] = wu_ref[...].astype(BF16)

    @pl.when(m < nu_ref[0])
    def _():
        x = x_ref[...]
        g = jnp.dot(x, wg_bf[...], preferred_element_type=F32)
        u = jnp.dot(x, wu_bf[...], preferred_element_type=F32)
        o_ref[...] = (g * _sigmoid(g) * u).astype(o_ref.dtype)

    @pl.when(m >= nu_ref[0])
    def _():
        o_ref[...] = jnp.zeros_like(o_ref)


def _moe_down_kernel(te_ref, nu_ref, a_ref, w_ref, o_ref, w_bf):
    m = pl.program_id(1)

    @pl.when(_tile_is_new(te_ref, m))
    def _():
        w_bf[...] = w_ref[...].astype(BF16)

    @pl.when(m < nu_ref[0])
    def _():
        o_ref[...] = jnp.dot(a_ref[...], w_bf[...], preferred_element_type=F32)

    @pl.when(m >= nu_ref[0])
    def _():
        o_ref[...] = jnp.zeros_like(o_ref)


def _moe_experts(xg, tile_expert, n_used, w, i):
    P, K = xg.shape
    F, TM = D_FF_EXPERT, MOE_TM
    NT = P // TM
    tn = 512
    nb = F // tn
    row = lambda j, m, te, nu: (jnp.minimum(m, nu[0] - 1), 0)
    out = lambda j, m, te, nu: (m, j)
    act = pl.pallas_call(
        _moe_up_kernel,
        grid_spec=pltpu.PrefetchScalarGridSpec(
            num_scalar_prefetch=2, grid=(nb, NT),
            in_specs=[pl.BlockSpec((TM, K), row),
                      pl.BlockSpec((None, None, K, tn), lambda j, m, te, nu: (i, te[m], 0, j)),
                      pl.BlockSpec((None, None, K, tn), lambda j, m, te, nu: (i, te[m], 0, j + nb))],
            out_specs=pl.BlockSpec((TM, tn), out),
            scratch_shapes=[pltpu.VMEM((K, tn), BF16), pltpu.VMEM((K, tn), BF16)]),
        out_shape=jax.ShapeDtypeStruct((P, F), BF16),
        compiler_params=_cp("arbitrary", "arbitrary"), name="moe_up",
    )(tile_expert, n_used, xg, w['moe_w_gu'], w['moe_w_gu'])
    tn = _pick_tn(F, K)
    return pl.pallas_call(
        _moe_down_kernel,
        grid_spec=pltpu.PrefetchScalarGridSpec(
            num_scalar_prefetch=2, grid=(K // tn, NT),
            in_specs=[pl.BlockSpec((TM, F), row),
                      pl.BlockSpec((None, None, F, tn), lambda j, m, te, nu: (i, te[m], 0, j))],
            out_specs=pl.BlockSpec((TM, tn), out),
            scratch_shapes=[pltpu.VMEM((F, tn), BF16)]),
        out_shape=jax.ShapeDtypeStruct((P, K), F32),
        compiler_params=_cp("arbitrary", "arbitrary"), name="moe_down",
    )(tile_expert, n_used, act, w['moe_w_down'])


def _combine_kernel(pos_ref, yg_hbm, route_ref, o_ref, buf, sem, *, M, TR):
    base = pl.program_id(0) * TR
    _lagged_copies(TR, lambda r: [_row_copy(yg_hbm, pos_ref[k * M + base + r], buf.at[k], r, sem.at[k])
                                  for k in range(TOP_K)], DMA_LAG)
    g1 = route_ref[:, 2 * N_EXPERTS:2 * N_EXPERTS + 1]
    g2 = route_ref[:, 2 * N_EXPERTS + 1:2 * N_EXPERTS + 2]
    o_ref[...] = g1 * buf[0] + g2 * buf[1]


def _combine(yg, pos, route):
    M = route.shape[0]
    D = yg.shape[1]
    TR = _pick_rows(M)
    grid_spec = pltpu.PrefetchScalarGridSpec(
        num_scalar_prefetch=1, grid=(M // TR,),
        in_specs=[pl.BlockSpec(memory_space=pl.ANY), pl.BlockSpec((TR, LANE), lambda t, pos: (t, 0))],
        out_specs=pl.BlockSpec((TR, D), lambda t, pos: (t, 0)),
        scratch_shapes=[pltpu.VMEM((TOP_K, TR, D), F32), pltpu.SemaphoreType.DMA((TOP_K,))])
    return pl.pallas_call(
        functools.partial(_combine_kernel, M=M, TR=TR), grid_spec=grid_spec,
        out_shape=jax.ShapeDtypeStruct((M, D), F32),
        compiler_params=_cp("arbitrary"), name="moe_combine",
    )(pos, yg, route)


def _moe(h, w, i):
    M, D = h.shape
    n_tiles = -(-(TOP_K * M + N_EXPERTS * (MOE_TM - 1)) // MOE_TM)
    w_r = jnp.pad(w['moe_router'][i], ((0, 0), (0, LANE - N_EXPERTS)))
    route = _router(h, w_r)
    dest, src, tile_expert, n_used = _moe_plan(route, n_tiles)
    xg = _dispatch(h, src)
    yg = _moe_experts(xg, tile_expert, n_used, w, i)
    return _combine(yg, dest, route)


def _trunk(xs, mods, pasts, w):
    S = range(len(xs))
    news = [{} for _ in S]
    depth = mods[0].shape[0]
    ng = w['norm_g']
    ffn_dtype = lambda layer: BF16 if layer % 2 == 0 else F32
    hs = [_norm(xs[s], ng[0, 0], nmod=mods[s][0], mod_rows=(0, 1)) for s in S]
    outs = [None for _ in S]
    for layer in range(depth):
        kind = layer % 4
        ys = []
        for s in S:
            h, past, new = hs[s], pasts[s], news[s]
            if kind == 0:
                y, new['mlstm'] = _mlstm(h, *past['mlstm'], w)
            elif kind == 1:
                y, new['sb'] = _sb_mixer(h, past['sb'], w)
            elif kind == 2:
                y, new['dw'] = _dw_mixer(h, past['dw'], w)
            else:
                y, new['gdn'] = _gdn_mixer(h, *past['gdn'], w)
            ys.append(y)
        for s in S:
            xs[s], hs[s] = _norm(xs[s], ng[layer, 1], y=ys[s], gmod=mods[s][layer], gate_row=2,
                                 nmod=mods[s][layer], mod_rows=(3, 4), h_dtype=ffn_dtype(layer))
        if layer % 2 == 0:
            ys = [_ffn(hs[s], w, layer // 2) for s in S]
        else:
            sizes = [hs[s].shape[0] * hs[s].shape[1] for s in S]
            y_all = _moe(jnp.concatenate([hs[s].reshape(sizes[s], D_MODEL) for s in S], axis=0), w, layer // 2)
            offs = np.cumsum([0] + sizes)
            ys = [y_all[offs[s]:offs[s + 1]].reshape(hs[s].shape) for s in S]
        for s in S:
            if layer + 1 < depth:
                xs[s], hs[s] = _norm(xs[s], ng[layer + 1, 0], y=ys[s], gmod=mods[s][layer], gate_row=5,
                                     nmod=mods[s][layer + 1], mod_rows=(0, 1))
            else:
                outs[s] = _norm(xs[s], w['final_g'], y=ys[s], gmod=mods[s][layer], gate_row=5, h_dtype=F32)
    return outs, news


def kernel(x_prompt, x_sample, c_prompt, c_sample, state_mlstm_C, state_mlstm_n, state_mlstm_m, cache_kv_sb, page_table, cache_kv_dw1, cache_kv_dw2, cache_kv_dw3, state_conv_gdn, state_S_gdn, w_ada, b_ada, norm_g, final_g, mlstm_w_in, mlstm_b_if, mlstm_norm_g, mlstm_w_out, sb_w_qkv, sb_w_out, sb_bias, dw_w_qkv, dw_w_out, rel_bias, gdn_w_in, gdn_conv_w, gdn_A_log, gdn_dt_bias, gdn_norm_g, gdn_w_out, ffn_w_gu, ffn_w_down, moe_router, moe_w_gu, moe_w_down):
    w = dict(norm_g=norm_g, final_g=final_g, mlstm_w_in=mlstm_w_in, mlstm_b_if=mlstm_b_if,
             mlstm_norm_g=mlstm_norm_g, mlstm_w_out=mlstm_w_out, sb_w_qkv=sb_w_qkv, sb_w_out=sb_w_out,
             sb_bias=sb_bias, dw_w_qkv=dw_w_qkv, dw_w_out=dw_w_out, rel_bias=rel_bias, gdn_w_in=gdn_w_in,
             gdn_conv_w=gdn_conv_w, gdn_A_log=gdn_A_log, gdn_dt_bias=gdn_dt_bias, gdn_norm_g=gdn_norm_g,
             gdn_w_out=gdn_w_out, ffn_w_gu=ffn_w_gu, ffn_w_down=ffn_w_down, moe_router=moe_router,
             moe_w_gu=moe_w_gu, moe_w_down=moe_w_down)
    Bp, Bd = x_prompt.shape[0], x_sample.shape[0]
    depth = w_ada.shape[0]
    rows = -(-(Bp + Bd) // 8) * 8
    c_all = jnp.pad(jnp.concatenate([c_prompt, c_sample], axis=0), ((0, rows - Bp - Bd), (0, 0)))
    mod = _ada(c_all, w_ada, b_ada).reshape(depth, rows, 6, D_MODEL)
    past_p = {
        'mlstm': (jnp.zeros((Bp, A_HEADS, A_DK, A_DV), F32), jnp.zeros((Bp, A_HEADS, A_DK), F32),
                  jnp.zeros((Bp, A_HEADS), F32)),
        'sb': None,
        'dw': None,
        'gdn': (jnp.zeros((Bp, D_CONV - 1, D_CONV_CH), F32), jnp.zeros((Bp, D_V_HEADS, D_DK, D_DV), F32)),
    }
    past_s = {
        'mlstm': (state_mlstm_C, state_mlstm_n, state_mlstm_m),
        'sb': (cache_kv_sb, page_table),
        'dw': (cache_kv_dw1, cache_kv_dw2, cache_kv_dw3),
        'gdn': (state_conv_gdn, state_S_gdn),
    }
    (y_prompt, y_sample), (new_p, new_s) = _trunk(
        [x_prompt, x_sample], [mod[:, :Bp], mod[:, Bp:Bp + Bd]], [past_p, past_s], w)
    C_p, n_p, m_p = new_p['mlstm']
    C_s, n_s, m_s = new_s['mlstm']
    dw1_p, dw2_p, dw3_p = new_p['dw']
    dw1_s, dw2_s, dw3_s = new_s['dw']
    conv_p, S_p = new_p['gdn']
    conv_s, S_s = new_s['gdn']
    return (y_prompt, y_sample, C_p, n_p, m_p, C_s, n_s, m_s, new_p['sb'], new_s['sb'],
            dw1_p, dw2_p, dw3_p, dw1_s, dw2_s, dw3_s, conv_p, S_p, conv_s, S_s)
```

```python
import functools
import math

import jax
import jax.numpy as jnp
import numpy as np
from jax import lax
from jax.experimental import pallas as pl
from jax.experimental.pallas import tpu as pltpu

F32 = jnp.float32
BF16 = jnp.bfloat16

D_MODEL = 2048
EPS = 1e-6
A_HEADS, A_DK, A_DV, A_CHUNK = 8, 128, 256, 64
B_HEADS, B_DH = 16, 128
Q_BLOCK = 128
C_GROUPS = ((128, 1), (512, 4), (2048, 16))
C_NG, C_HPG, C_DH = 3, 8, 128
N_BUCKETS, MAX_DISTANCE = 32, 2048
D_QK_HEADS, D_V_HEADS, D_DK, D_DV, D_CONV, D_CHUNK = 16, 32, 128, 128, 4, 64
D_CONV_CH = 2 * D_QK_HEADS * D_DK + D_V_HEADS * D_DV
D_FF, N_EXPERTS, TOP_K, D_FF_EXPERT = 5632, 8, 2, 7168
MOE_TM = 512
DMA_LAG = 32
DMA_UNROLL = 4
PAGE_SIZE = 128
LANE = 128
SB_TILE = 256
SB_HEADS_PER_STEP = 4
SB_PAGE_SLOTS = 4
DISPATCH_STRIDE = 67

VMEM_LIMIT_BYTES = 56 * 1024 * 1024
WEIGHT_BLOCK_BYTES = 12 * 1024 * 1024
NEG_BIG = -1e30


def _cp(*sem):
    return pltpu.CompilerParams(dimension_semantics=sem, vmem_limit_bytes=VMEM_LIMIT_BYTES)


def _dot(a, b):
    return jnp.dot(a.astype(BF16), b.astype(BF16), preferred_element_type=F32)


def _dot_nt(a, b):
    return lax.dot_general(a.astype(BF16), b.astype(BF16), (((1,), (1,)), ((), ())),
                           preferred_element_type=F32)


def _dot_tn(a, b):
    return lax.dot_general(a.astype(BF16), b.astype(BF16), (((0,), (0,)), ((), ())),
                           preferred_element_type=F32)


def _split(a):
    hi = a.astype(BF16)
    return hi, (a - hi.astype(F32)).astype(BF16)


def _dot_hi(a, b):
    a_hi, a_lo = _split(a)
    b_hi, b_lo = _split(b)
    d = lambda x, y: jnp.dot(x, y, preferred_element_type=F32)
    return d(a_hi, b_hi) + (d(a_hi, b_lo) + d(a_lo, b_hi))


def _sigmoid(x):
    return 1.0 / (1.0 + jnp.exp(-x))


def _log_sigmoid(x):
    return jnp.minimum(x, 0.0) - jnp.log1p(jnp.exp(-jnp.abs(x)))


def _softplus(x):
    return jnp.maximum(x, 0.0) + jnp.log1p(jnp.exp(-jnp.abs(x)))


def _iota2(shape, axis):
    return lax.broadcasted_iota(jnp.int32, shape, axis)


def _row_to_col(row, eye):
    return jnp.sum(jnp.where(eye, row, 0.0), axis=1, keepdims=True)


def _pick_tn(K, N, col0=0):
    for tn in (2048, 1024, 512, 256, 128):
        if N % tn == 0 and col0 % tn == 0 and K * tn * 4 <= WEIGHT_BLOCK_BYTES:
            return tn
    raise ValueError((K, N, col0))


def _pick_tm(M, K):
    if M % 1024 == 0 and K <= 2048:
        return 1024
    return 512 if M % 512 == 0 else M


def _pick_rows(M, cap=1024):
    best = M
    for t in range(8, min(M, cap) + 1, 8):
        if M % t == 0:
            best = t
    return best


def _linear_kernel(x_ref, w_ref, o_ref, wbf_ref):
    @pl.when(pl.program_id(1) == 0)
    def _():
        wbf_ref[...] = w_ref[...].astype(BF16)

    o_ref[...] = jnp.dot(x_ref[...], wbf_ref[...], preferred_element_type=F32).astype(o_ref.dtype)


def _linear(x, w, *, sel=(), col0=0, ncols=None, out_dtype=F32):
    M, K = x.shape
    assert w.shape[-2] == K
    N = w.shape[-1] - col0 if ncols is None else ncols
    tn, tm = _pick_tn(K, N, col0), _pick_tm(M, K)
    off = col0 // tn
    w_spec = pl.BlockSpec((None,) * len(sel) + (K, tn), lambda j, m: tuple(sel) + (0, j + off))
    return pl.pallas_call(
        _linear_kernel, grid=(N // tn, M // tm),
        in_specs=[pl.BlockSpec((tm, K), lambda j, m: (m, 0)), w_spec],
        out_specs=pl.BlockSpec((tm, tn), lambda j, m: (m, j)),
        out_shape=jax.ShapeDtypeStruct((M, N), out_dtype),
        scratch_shapes=[pltpu.VMEM((K, tn), BF16)],
        compiler_params=_cp("arbitrary", "arbitrary"), name="linear",
    )(x, w)


def _gu_kernel(x_ref, wg_ref, wu_ref, o_ref, wg_bf, wu_bf):
    @pl.when(pl.program_id(1) == 0)
    def _():
        wg_bf[...] = wg_ref[...].astype(BF16)
        wu_bf[...] = wu_ref[...].astype(BF16)

    x = x_ref[...]
    g = jnp.dot(x, wg_bf[...], preferred_element_type=F32)
    u = jnp.dot(x, wu_bf[...], preferred_element_type=F32)
    o_ref[...] = (g * _sigmoid(g) * u).astype(o_ref.dtype)


def _swiglu_up(x, w, sel, F):
    M, K = x.shape
    tn, tm = 512, _pick_tm(M, K)
    lead = (None,) * len(sel)
    nb = F // tn
    return pl.pallas_call(
        _gu_kernel, grid=(nb, M // tm),
        in_specs=[pl.BlockSpec((tm, K), lambda j, m: (m, 0)),
                  pl.BlockSpec(lead + (K, tn), lambda j, m: tuple(sel) + (0, j)),
                  pl.BlockSpec(lead + (K, tn), lambda j, m: tuple(sel) + (0, j + nb))],
        out_specs=pl.BlockSpec((tm, tn), lambda j, m: (m, j)),
        out_shape=jax.ShapeDtypeStruct((M, F), BF16),
        scratch_shapes=[pltpu.VMEM((K, tn), BF16), pltpu.VMEM((K, tn), BF16)],
        compiler_params=_cp("arbitrary", "arbitrary"), name="swiglu_up",
    )(x, w, w)


def _router_kernel(x_ref, w_ref, o_ref):
    E = N_EXPERTS
    logits = jnp.dot(x_ref[...].astype(BF16), w_ref[...].astype(BF16), preferred_element_type=F32)
    lane = _iota2(logits.shape, 1)
    logits = jnp.where(lane < E, logits, -jnp.inf)
    m1 = jnp.max(logits, axis=1, keepdims=True)
    i1 = jnp.min(jnp.where(logits == m1, lane, LANE), axis=1, keepdims=True)
    rest = jnp.where(lane == i1, -jnp.inf, logits)
    m2 = jnp.max(rest, axis=1, keepdims=True)
    i2 = jnp.min(jnp.where(rest == m2, lane, LANE), axis=1, keepdims=True)
    e2 = jnp.exp(m2 - m1)
    g1 = 1.0 / (1.0 + e2)
    g2 = e2 / (1.0 + e2)
    o_ref[...] = (jnp.where(lane == i1, 1.0, 0.0) + jnp.where(lane == i2 + E, 1.0, 0.0)
                  + jnp.where(lane == 2 * E, g1, 0.0) + jnp.where(lane == 2 * E + 1, g2, 0.0))


def _router(x, w_pad):
    M, K = x.shape
    tm = _pick_rows(M)
    return pl.pallas_call(
        _router_kernel, grid=(M // tm,),
        in_specs=[pl.BlockSpec((tm, K), lambda m: (m, 0)), pl.BlockSpec((K, LANE), lambda m: (0, 0))],
        out_specs=pl.BlockSpec((tm, LANE), lambda m: (m, 0)),
        out_shape=jax.ShapeDtypeStruct((M, LANE), F32),
        compiler_params=_cp("arbitrary"), name="router",
    )(x, w_pad)


def _ada_kernel(c_ref, w_ref, b_ref, o_ref):
    c = c_ref[...]
    x = (c * _sigmoid(c)).astype(BF16)
    o_ref[0] = jnp.dot(x, w_ref[...].astype(BF16), preferred_element_type=F32) + b_ref[0]


def _ada(c_pad, w_ada, b_ada):
    R = c_pad.shape[0]
    depth, K, N = w_ada.shape
    tn = 1024
    return pl.pallas_call(
        _ada_kernel, grid=(depth, N // tn),
        in_specs=[pl.BlockSpec((R, K), lambda l, j: (0, 0)),
                  pl.BlockSpec((None, K, tn), lambda l, j: (l, 0, j)),
                  pl.BlockSpec((1, 1, tn), lambda l, j: (l, 0, j))],
        out_specs=pl.BlockSpec((1, R, tn), lambda l, j: (l, 0, j)),
        out_shape=jax.ShapeDtypeStruct((depth, R, N), F32),
        compiler_params=_cp("arbitrary", "arbitrary"), name="ada",
    )(c_pad, w_ada, b_ada.reshape(depth, 1, N))


def _norm_kernel(*refs, has_res, mod_rows, gate_row):
    refs = list(refs)
    x_ref = refs.pop(0)
    x = x_ref[0]
    if has_res:
        y_ref, gmod_ref = refs.pop(0), refs.pop(0)
        x = x + gmod_ref[0, gate_row:gate_row + 1, :] * y_ref[0]
    g_ref = refs.pop(0)
    nmod_ref = refs.pop(0) if mod_rows is not None else None
    if has_res and mod_rows is not None:
        xo_ref = refs.pop(0)
        xo_ref[0] = x
    h_ref = refs.pop(0)
    y = x * lax.rsqrt(jnp.mean(x * x, axis=-1, keepdims=True) + EPS) * g_ref[...]
    if mod_rows is not None:
        shift_row, scale_row = mod_rows
        y = y * (1.0 + nmod_ref[0, scale_row:scale_row + 1, :]) + nmod_ref[0, shift_row:shift_row + 1, :]
    h_ref[0] = y.astype(h_ref.dtype)


def _norm(x, g, *, y=None, gmod=None, gate_row=None, nmod=None, mod_rows=None, h_dtype=BF16):
    B, T, D = x.shape
    tt = min(T, 256)
    has_res = y is not None
    xs = pl.BlockSpec((1, tt, D), lambda b, t: (b, t, 0))
    ms = pl.BlockSpec((1, 6, D), lambda b, t: (b, 0, 0))
    args, specs = [x], [xs]
    if has_res:
        args += [y, gmod]
        specs += [xs, ms]
    args.append(g.reshape(1, D))
    specs.append(pl.BlockSpec((1, D), lambda b, t: (0, 0)))
    if mod_rows is not None:
        args.append(nmod)
        specs.append(ms)
    out_shape = [jax.ShapeDtypeStruct((B, T, D), h_dtype)]
    out_specs = [xs]
    if has_res and mod_rows is not None:
        out_shape.insert(0, jax.ShapeDtypeStruct((B, T, D), F32))
        out_specs.insert(0, xs)
    out = pl.pallas_call(
        functools.partial(_norm_kernel, has_res=has_res, mod_rows=mod_rows, gate_row=gate_row),
        grid=(B, T // tt), in_specs=specs, out_specs=out_specs, out_shape=out_shape,
        compiler_params=_cp("arbitrary", "arbitrary"), name="norm",
    )(*args)
    return out if len(out) > 1 else out[0]


def _mlstm_kernel(bif_ref, q_ref, k_ref, v_ref, og_ref, gi_ref, gf_ref, c0_ref, n0_ref, m0_ref, gn_ref,
                  hs_ref, c_out, n_out, m_out, C_s, n_s, m_s, *, L, nC):
    c = pl.program_id(1)
    H, DK, DV = A_HEADS, A_DK, A_DV
    heads = range(H)

    @pl.when(c == 0)
    def _():
        C_s[...] = c0_ref[0].reshape(H * DK, DV)
        n_s[...] = n0_ref[0]
        m_s[...] = m0_ref[0]

    row, col = _iota2((L, L), 0), _iota2((L, L), 1)
    eye, tril = row == col, col <= row
    qs = [q_ref[0, :, hd * DK:(hd + 1) * DK] * (DK ** -0.5) for hd in heads]
    ks = [k_ref[0, :, hd * DK:(hd + 1) * DK] for hd in heads]
    vs = [v_ref[0, :, hd * DV:(hd + 1) * DV] for hd in heads]
    Cs = [C_s[hd * DK:(hd + 1) * DK, :] for hd in heads]
    qks = [_dot_nt(qs[hd], ks[hd]) for hd in heads]
    qCs = [_dot(qs[hd], Cs[hd]) for hd in heads]
    m_ts, scs, Ds, decs, w_cols, m_news = [], [], [], [], [], []
    for hd in heads:
        li_row = gi_ref[hd, pl.ds(c, 1), :] + bif_ref[0, hd]
        lf_row = _log_sigmoid(gf_ref[hd, pl.ds(c, 1), :] + bif_ref[1, hd])
        li_col, lf_col = _row_to_col(li_row, eye), _row_to_col(lf_row, eye)
        b_col = jnp.sum(jnp.where(tril, lf_row, 0.0), axis=1, keepdims=True)
        b_row = jnp.sum(jnp.where(row <= col, lf_col, 0.0), axis=0, keepdims=True)
        b_last = jnp.sum(lf_row, axis=1, keepdims=True)
        m_prev = m_s[hd:hd + 1, :]
        dlog = jnp.where(tril, b_col - b_row + li_row, -jnp.inf)
        inter = b_col + m_prev
        m_t = jnp.maximum(inter, jnp.max(dlog, axis=1, keepdims=True))
        m_new = jnp.max(jnp.where(row[:, :1] == L - 1, m_t, -jnp.inf), axis=0, keepdims=True)
        m_ts.append(m_t)
        scs.append(jnp.exp(inter - m_t))
        Ds.append(jnp.exp(dlog - m_t))
        decs.append(jnp.exp(b_last + m_prev - m_new))
        w_cols.append(jnp.exp(b_last - b_col + li_col - m_new))
        m_news.append(m_new)
    s_qks = [qks[hd] * Ds[hd] for hd in heads]
    nums = [_dot(s_qks[hd], vs[hd]) + scs[hd] * qCs[hd] for hd in heads]
    kws = [ks[hd] * w_cols[hd] for hd in heads]
    C_new = jnp.concatenate([decs[hd] * Cs[hd] + _dot_tn(kws[hd], vs[hd]) for hd in heads], axis=0)
    row8 = _iota2((H, 1), 0)
    n_new, m_new, outs = jnp.zeros((H, DK), F32), jnp.zeros((H, 1), F32), []
    for hd in heads:
        n = n_s[hd:hd + 1, :]
        den = jnp.sum(s_qks[hd], axis=1, keepdims=True) + scs[hd] * jnp.sum(qs[hd] * n, axis=1, keepdims=True)
        hc = nums[hd] / jnp.maximum(jnp.abs(den), jnp.exp(-m_ts[hd]))
        hn = hc * lax.rsqrt(jnp.mean(hc * hc, axis=-1, keepdims=True) + EPS) * gn_ref[hd:hd + 1, :]
        outs.append((hn * _sigmoid(og_ref[0, :, hd * DV:(hd + 1) * DV])).astype(hs_ref.dtype))
        n_new = n_new + jnp.where(row8 == hd, decs[hd] * n + jnp.sum(kws[hd], axis=0, keepdims=True), 0.0)
        m_new = m_new + jnp.where(row8 == hd, m_news[hd], 0.0)
    C_s[...] = C_new
    n_s[...] = n_new
    m_s[...] = m_new
    hs_ref[0] = jnp.concatenate(outs, axis=1)

    @pl.when(c == nC - 1)
    def _():
        c_out[0] = C_new.reshape(H, DK, DV)
        n_out[0] = n_new
        m_out[0] = m_new


def _mlstm(h, C0, n0, m0, w):
    B, T, D = h.shape
    H, DK, DV = A_HEADS, A_DK, A_DV
    h2 = h.reshape(B * T, D)
    n_main = 2 * H * DK + 2 * H * DV
    proj = _linear(h2, w['mlstm_w_in'], ncols=n_main).reshape(B, T, n_main)
    w_gate = jnp.pad(w['mlstm_w_in'][:, n_main:], ((0, 0), (0, LANE - 2 * H)))
    gates = _linear(h2, w_gate)
    L = math.gcd(T, A_CHUNK)
    nC = T // L
    to_rows = lambda a: a.reshape(B, nC, L, H).transpose(0, 3, 1, 2).reshape(B * H, nC, L)
    gi, gf = to_rows(gates[:, :H]), to_rows(gates[:, H:2 * H])
    wk, wv = H * DK, H * DV
    cols = lambda wd, off: pl.BlockSpec((1, L, wd), lambda b, c: (b, c, off))
    gs = pl.BlockSpec((H, nC, L), lambda b, c: (b, 0, 0))
    cs = pl.BlockSpec((1, H, DK, DV), lambda b, c: (b, 0, 0, 0))
    ns = pl.BlockSpec((1, H, DK), lambda b, c: (b, 0, 0))
    ms = pl.BlockSpec((1, H, 1), lambda b, c: (b, 0, 0))
    hs, C, n, m = pl.pallas_call(
        functools.partial(_mlstm_kernel, L=L, nC=nC), grid=(B, nC),
        in_specs=[pl.BlockSpec(memory_space=pltpu.SMEM), cols(wk, 0), cols(wk, 1), cols(wv, 2 * wk // wv),
                  cols(wv, 2 * wk // wv + 1), gs, gs, cs, ns, ms, pl.BlockSpec((H, DV), lambda b, c: (0, 0))],
        out_specs=[cols(wv, 0), cs, ns, ms],
        out_shape=[jax.ShapeDtypeStruct((B, T, H * DV), BF16), jax.ShapeDtypeStruct((B, H, DK, DV), F32),
                   jax.ShapeDtypeStruct((B, H, DK), F32), jax.ShapeDtypeStruct((B, H, 1), F32)],
        scratch_shapes=[pltpu.VMEM((H * DK, DV), F32), pltpu.VMEM((H, DK), F32), pltpu.VMEM((H, 1), F32)],
        compiler_params=_cp("arbitrary", "arbitrary"), name="mlstm",
    )(w['mlstm_b_if'], proj, proj, proj, proj, gi, gf, C0.astype(F32), n0.astype(F32),
      m0.astype(F32).reshape(B, H, 1), w['mlstm_norm_g'].astype(F32))
    y = _linear(hs.reshape(B * T, H * DV), w['mlstm_w_out'])
    return y.reshape(B, T, D), (C, n, m.reshape(B, H))


def _sb_weights(zs, valid, laters, upper):
    lss = [_log_sigmoid(z) for z in zs]
    l1s = [ls - z for ls, z in zip(lss, zs)]
    if valid is not None:
        l1s = [jnp.where(valid, l1, 0.0) for l1 in l1s]
    his = [l1.astype(BF16) for l1 in l1s]
    los = [(l1 - hi.astype(F32)).astype(BF16) for l1, hi in zip(l1s, his)]
    afters = [jnp.dot(hi, upper, preferred_element_type=F32) + jnp.dot(lo, upper, preferred_element_type=F32)
              for hi, lo in zip(his, los)]
    as_ = [jnp.exp(ls + after + later) for ls, after, later in zip(lss, afters, laters)]
    if valid is not None:
        as_ = [jnp.where(valid, a, 0.0) for a in as_]
    return as_, [later + jnp.sum(l1, axis=1, keepdims=True) for later, l1 in zip(laters, l1s)]


def _sb_kernel(bias_ref, q_ref, k_ref, v_ref, o_ref, *, TQ, HG):
    hg, i = pl.program_id(1), pl.program_id(2)
    DH = B_DH
    heads = range(HG)
    cols = lambda hd: slice(hd * DH, (hd + 1) * DH)
    qs = [q_ref[0, :, cols(hd)] * (DH ** -0.5) for hd in heads]
    biases = [bias_ref[hg * HG + hd] for hd in heads]
    row, col = _iota2((TQ, TQ), 0), _iota2((TQ, TQ), 1)
    upper = jnp.where(row > col, 1.0, 0.0).astype(BF16)

    def tile(kk, carry, valid):
        outs, laters = carry
        keys = pl.ds(pl.multiple_of((i - kk) * TQ, TQ), TQ)
        zs = [_dot_nt(qs[hd], k_ref[0, keys, cols(hd)]) + biases[hd] for hd in heads]
        as_, laters = _sb_weights(zs, valid, laters, upper)
        return [outs[hd] + _dot(as_[hd], v_ref[0, keys, cols(hd)]) for hd in heads], laters

    init = ([jnp.zeros((TQ, DH), F32) for _ in heads], [jnp.zeros((TQ, 1), F32) for _ in heads])
    carry = tile(0, init, col < row)
    outs, _ = lax.fori_loop(1, i + 1, lambda kk, c: tile(kk, c, None), carry)
    o_ref[0] = jnp.concatenate(outs, axis=1).astype(o_ref.dtype)


def _sb_prompt(proj, bias):
    B, T, _ = proj.shape
    H, HG = B_HEADS, SB_HEADS_PER_STEP
    TQ = math.gcd(T, SB_TILE)
    wd = HG * B_DH
    kvs = lambda off: pl.BlockSpec((1, T, wd), lambda b, hg, i: (b, 0, off + hg))
    qo = pl.BlockSpec((1, TQ, wd), lambda b, hg, i: (b, i, hg))
    return pl.pallas_call(
        functools.partial(_sb_kernel, TQ=TQ, HG=HG), grid=(B, H // HG, T // TQ),
        in_specs=[pl.BlockSpec(memory_space=pltpu.SMEM), qo, kvs(H // HG), kvs(2 * H // HG)],
        out_specs=qo,
        out_shape=jax.ShapeDtypeStruct((B, T, H * B_DH), BF16),
        compiler_params=_cp("arbitrary", "arbitrary", "arbitrary"), name="sb_prompt",
    )(bias, proj, proj, proj)


def _sb_dec_kernel(pt_ref, q_ref, bias_ref, kvn_ref, cache_hbm, o_ref, kv_buf, sem, acc_s, later_s,
                   *, n_seq, n_pages, TN):
    b, p = pl.program_id(0), pl.program_id(1)
    step = b * n_pages + p
    n_steps = n_seq * n_pages
    n_slots = kv_buf.shape[0]
    ahead = n_slots - 1
    slot = lax.rem(step, n_slots)
    H = B_HEADS
    R = H * TN

    def page_copies(s, sl):
        page = pt_ref[s // n_pages, n_pages - 1 - s % n_pages]
        return [pltpu.make_async_copy(cache_hbm.at[page, :, kv, hd, :], kv_buf.at[sl, kv * H + hd], sem.at[sl])
                for kv in range(2) for hd in range(H)]

    @pl.when(step == 0)
    def _():
        for s in range(min(ahead, n_steps)):
            for cp in page_copies(s, s % n_slots):
                cp.start()

    @pl.when(step + ahead < n_steps)
    def _():
        for cp in page_copies(step + ahead, lax.rem(step + ahead, n_slots)):
            cp.start()

    bias = bias_ref[...]
    row, col = _iota2((PAGE_SIZE, PAGE_SIZE), 0), _iota2((PAGE_SIZE, PAGE_SIZE), 1)
    upper = jnp.where(row > col, 1.0, 0.0).astype(BF16)

    def segment(rows_of, valid, later):
        z = jnp.concatenate([_dot_nt(q_ref[0, hd], rows_of(hd)) for hd in range(H)], axis=0) + bias
        (a,), (later,) = _sb_weights([z], valid, [later], upper)
        out = jnp.concatenate([_dot(a[hd * TN:(hd + 1) * TN], rows_of(H + hd)) for hd in range(H)], axis=0)
        return out, later

    @pl.when(p == 0)
    def _():
        rq, ck = _iota2((R, PAGE_SIZE), 0), _iota2((R, PAGE_SIZE), 1)
        out, later = segment(lambda i: kvn_ref[0, i], ck < lax.rem(rq, TN), jnp.zeros((R, 1), F32))
        acc_s[...] = out
        later_s[...] = later

    for cp in page_copies(step, slot):
        cp.wait()
    out, later = segment(lambda i: kv_buf[slot, i], None, later_s[...])
    acc_s[...] += out
    later_s[...] = later

    @pl.when(p == n_pages - 1)
    def _():
        o_ref[0] = acc_s[...]


def _sb_decode(proj, cache, page_table, bias):
    B, TN, _ = proj.shape
    H, DH = B_HEADS, B_DH
    HD = H * DH
    R = H * TN
    n_pages = page_table.shape[1]
    q = proj[:, :, :HD].reshape(B, TN, H, DH).transpose(0, 2, 1, 3) * (DH ** -0.5)
    kvn = proj[:, :, HD:].reshape(B, TN, 2 * H, DH).transpose(0, 2, 1, 3)
    kvn = jnp.pad(kvn, ((0, 0), (0, 0), (0, PAGE_SIZE - TN), (0, 0)))
    bias_col = jnp.repeat(bias.astype(F32), TN).reshape(R, 1)
    grid_spec = pltpu.PrefetchScalarGridSpec(
        num_scalar_prefetch=1, grid=(B, n_pages),
        in_specs=[pl.BlockSpec((1, H, TN, DH), lambda b, p, pt: (b, 0, 0, 0)),
                  pl.BlockSpec((R, 1), lambda b, p, pt: (0, 0)),
                  pl.BlockSpec((1, 2 * H, PAGE_SIZE, DH), lambda b, p, pt: (b, 0, 0, 0)),
                  pl.BlockSpec(memory_space=pl.ANY)],
        out_specs=pl.BlockSpec((1, R, DH), lambda b, p, pt: (b, 0, 0)),
        scratch_shapes=[pltpu.VMEM((SB_PAGE_SLOTS, 2 * H, PAGE_SIZE, DH), F32),
                        pltpu.SemaphoreType.DMA((SB_PAGE_SLOTS,)),
                        pltpu.VMEM((R, DH), F32), pltpu.VMEM((R, 1), F32)])
    o = pl.pallas_call(
        functools.partial(_sb_dec_kernel, n_seq=B, n_pages=n_pages, TN=TN), grid_spec=grid_spec,
        out_shape=jax.ShapeDtypeStruct((B, R, DH), F32),
        compiler_params=_cp("arbitrary", "arbitrary"), name="sb_decode",
    )(page_table, q, bias_col, kvn, cache)
    return o.reshape(B, H, TN, DH).transpose(0, 2, 1, 3).reshape(B, TN, HD).astype(BF16)


def _sb_mixer(h, past, w):
    B, T, D = h.shape
    proj = _linear(h.reshape(B * T, D), w['sb_w_qkv']).reshape(B, T, 3 * B_HEADS * B_DH)
    if past is None:
        o = _sb_prompt(proj, w['sb_bias'])
    else:
        o = _sb_decode(proj, past[0], past[1], w['sb_bias'])
    y = _linear(o.reshape(B * T, B_HEADS * B_DH), w['sb_w_out'])
    kv_new = proj[:, :, B_HEADS * B_DH:].reshape(B, T, 2, B_HEADS, B_DH)
    return y.reshape(B, T, D), kv_new


def _t5_bucket(dist):
    max_exact = N_BUCKETS // 2
    large = max_exact + (jnp.log(jnp.maximum(dist, 1).astype(F32) / max_exact)
                         / math.log(MAX_DISTANCE / max_exact) * (N_BUCKETS - max_exact)).astype(jnp.int32)
    return jnp.where(dist < max_exact, dist, jnp.minimum(large, N_BUCKETS - 1))


def _tap_bias(rel_bias, g, taps, valid):
    win, dil = C_GROUPS[g]
    J = win // dil + 1
    valid = jnp.logical_and(valid, jnp.logical_and(taps >= 0, taps < J))
    tab = rel_bias[_t5_bucket(dil * jnp.arange(J))][:, g * C_HPG:(g + 1) * C_HPG].astype(F32)
    hit = taps[None, ..., None] == jnp.arange(J)
    vals = jnp.sum(jnp.where(hit, tab.T.reshape((C_HPG,) + (1,) * taps.ndim + (J,)), 0.0), axis=-1)
    return jnp.where(valid[None], vals, -jnp.inf)


def _band_kernel(q_ref, kp_ref, kc_ref, vp_ref, vc_ref, bias_ref, o_ref, lse_ref, *, TQ):
    i = pl.program_id(2)
    H, DH = C_HPG, C_DH
    heads = range(H)
    head = lambda ref, hd: ref[0, :, hd * DH:(hd + 1) * DH]
    qs = [head(q_ref, hd) * (DH ** -0.5) for hd in heads]
    s_cs = [_dot_nt(qs[hd], head(kc_ref, hd)) + bias_ref[hd, :, TQ:] for hd in heads]
    s_ps = [jnp.where(i > 0, _dot_nt(qs[hd], head(kp_ref, hd)) + bias_ref[hd, :, :TQ], -jnp.inf) for hd in heads]
    mxs = [jnp.maximum(jnp.max(s_cs[hd], axis=1, keepdims=True), jnp.max(s_ps[hd], axis=1, keepdims=True))
           for hd in heads]
    p_cs = [jnp.exp(s_cs[hd] - mxs[hd]) for hd in heads]
    p_ps = [jnp.exp(s_ps[hd] - mxs[hd]) for hd in heads]
    ls = [jnp.sum(p_cs[hd], axis=1, keepdims=True) + jnp.sum(p_ps[hd], axis=1, keepdims=True) for hd in heads]
    outs = [(_dot(p_cs[hd], head(vc_ref, hd)) + _dot(p_ps[hd], head(vp_ref, hd))) / ls[hd] for hd in heads]
    o_ref[0] = jnp.concatenate(outs, axis=1)
    lane = _iota2((TQ, H), 1)
    lse = jnp.zeros((TQ, H), F32)
    for hd in heads:
        lse = lse + jnp.where(lane == hd, mxs[hd] + jnp.log(ls[hd]), 0.0)
    lse_ref[0, 0] = lse


def _dw_prompt_group(proj, rel_bias, g):
    B, T, W3 = proj.shape
    win, dil = C_GROUPS[g]
    H, DH = C_HPG, C_DH
    TQ = win // dil
    Ts = T // dil
    assert Ts % TQ == 0
    HD = H * DH
    nb = 3
    cols = [proj[:, :, (which * C_NG + g) * HD:(which * C_NG + g + 1) * HD] for which in range(nb)]
    pv = jnp.concatenate(cols, axis=2).reshape(B, Ts, dil * nb * HD)
    t_loc, s_loc = jnp.arange(TQ)[:, None], jnp.arange(2 * TQ)[None, :] - TQ
    bias = _tap_bias(rel_bias, g, t_loc - s_loc, jnp.ones((TQ, 2 * TQ), bool))
    blk = lambda which, prev: pl.BlockSpec(
        (1, TQ, HD), lambda b, r, i: (b, jnp.maximum(i - 1, 0) if prev else i, r * nb + which))
    o, lse = pl.pallas_call(
        functools.partial(_band_kernel, TQ=TQ), grid=(B, dil, Ts // TQ),
        in_specs=[blk(0, False), blk(1, True), blk(1, False), blk(2, True), blk(2, False),
                  pl.BlockSpec((H, TQ, 2 * TQ), lambda b, r, i: (0, 0, 0))],
        out_specs=[pl.BlockSpec((1, TQ, HD), lambda b, r, i: (b, i, r)),
                   pl.BlockSpec((1, 1, TQ, H), lambda b, r, i: (b, r, i, 0))],
        out_shape=[jax.ShapeDtypeStruct((B, Ts, dil * HD), F32),
                   jax.ShapeDtypeStruct((B, dil, Ts, H), F32)],
        compiler_params=_cp("arbitrary", "arbitrary", "arbitrary"), name="dw_band",
    )(pv, pv, pv, pv, pv, bias)
    lse = lse.transpose(0, 2, 1, 3).reshape(B, T, H)
    return o.reshape(B, T, H * DH), lse


def _dw_dec_kernel(q_ref, bmn_ref, bm_ref, kvn_ref, buf_ref, o_ref, lse_ref, m_s, l_s, acc_s, *, n_tiles, TN):
    wi = pl.program_id(1)
    H = C_HPG
    R = H * TN

    def segment(src_ref, bm, m_old, l_old, acc_old):
        n_keys = bm.shape[1]
        head_rows = lambda first: src_ref[pl.ds(0, 1), pl.ds(first, n_keys, stride=2 * H), :][0]
        s = jnp.concatenate([_dot_nt(q_ref[0, hd], head_rows(hd)) for hd in range(H)], axis=0) + bm
        m_new = jnp.maximum(m_old, jnp.max(s, axis=1, keepdims=True))
        alpha = jnp.exp(m_old - m_new)
        p = jnp.exp(s - m_new)
        pv = jnp.concatenate([_dot(p[hd * TN:(hd + 1) * TN], head_rows(H + hd)) for hd in range(H)], axis=0)
        m_s[...] = m_new
        l_s[...] = alpha * l_old + jnp.sum(p, axis=1, keepdims=True)
        acc_s[...] = alpha * acc_old + pv

    @pl.when(wi == 0)
    def _():
        segment(kvn_ref, bmn_ref[...], jnp.full((R, 1), NEG_BIG, F32), jnp.zeros((R, 1), F32),
                jnp.zeros((R, C_DH), F32))

    segment(buf_ref, bm_ref[...], m_s[...], l_s[...], acc_s[...])

    @pl.when(wi == n_tiles - 1)
    def _():
        l = l_s[...]
        lse_ref[0] = m_s[...] + jnp.log(l)
        o_ref[0] = acc_s[...] / l


def _dw_decode_group(proj, buf, rel_bias, g):
    B, TN, _ = proj.shape
    win, dil = C_GROUPS[g]
    H, DH = C_HPG, C_DH
    HD = H * DH
    R = H * TN
    W = buf.shape[1]
    TW = min(W, 512)
    p6 = proj.reshape(B, TN, 3, C_NG, H, DH)
    q = p6[:, :, 0, g].transpose(0, 2, 1, 3) * (DH ** -0.5)
    kvn = jnp.pad(p6[:, :, 1:, g].reshape(B, TN * 2 * H, DH), ((0, 0), (0, (LANE - TN) * 2 * H), (0, 0)))
    t = jnp.arange(TN)[:, None]
    dist_buf = W + t - jnp.arange(W)[None, :]
    dist_new = t - jnp.arange(LANE)[None, :]
    bias_of = lambda dist, ok: _tap_bias(rel_bias, g, dist // dil, jnp.logical_and(ok, dist % dil == 0))
    bm = bias_of(dist_buf, jnp.ones_like(dist_buf, bool)).reshape(R, W)
    bmn = bias_of(dist_new, jnp.arange(LANE)[None, :] < TN).reshape(R, LANE)
    o, lse = pl.pallas_call(
        functools.partial(_dw_dec_kernel, n_tiles=W // TW, TN=TN), grid=(B, W // TW),
        in_specs=[pl.BlockSpec((1, H, TN, DH), lambda b, wi: (b, 0, 0, 0)),
                  pl.BlockSpec((R, LANE), lambda b, wi: (0, 0)),
                  pl.BlockSpec((R, TW), lambda b, wi: (0, wi)),
                  pl.BlockSpec((1, LANE * 2 * H, DH), lambda b, wi: (b, 0, 0)),
                  pl.BlockSpec((1, TW * 2 * H, DH), lambda b, wi: (b, wi, 0))],
        out_specs=[pl.BlockSpec((1, R, DH), lambda b, wi: (b, 0, 0)),
                   pl.BlockSpec((1, R, 1), lambda b, wi: (b, 0, 0))],
        out_shape=[jax.ShapeDtypeStruct((B, R, DH), F32), jax.ShapeDtypeStruct((B, R, 1), F32)],
        scratch_shapes=[pltpu.VMEM((R, 1), F32), pltpu.VMEM((R, 1), F32), pltpu.VMEM((R, DH), F32)],
        compiler_params=_cp("arbitrary", "arbitrary"), name="dw_decode",
    )(q, bmn, bm, kvn, buf.reshape(B, W * 2 * H, DH))
    o = o.reshape(B, H, TN, DH).transpose(0, 2, 1, 3).reshape(B, TN, HD)
    lse = lse.reshape(B, H, TN).transpose(0, 2, 1)
    return o, lse


def _dw_combine_kernel(o0, o1, o2, l0, l1, l2, out_ref):
    ls = [l0[0], l1[0], l2[0]]
    mx = jnp.maximum(jnp.maximum(ls[0], ls[1]), ls[2])
    es = [jnp.exp(l - mx) for l in ls]
    tot = es[0] + es[1] + es[2]
    ws = [e / tot for e in es]
    for hd in range(C_HPG):
        cols = slice(hd * C_DH, (hd + 1) * C_DH)
        acc = ws[0][:, hd:hd + 1] * o0[0, :, cols]
        acc = acc + ws[1][:, hd:hd + 1] * o1[0, :, cols]
        acc = acc + ws[2][:, hd:hd + 1] * o2[0, :, cols]
        out_ref[0, :, cols] = acc.astype(out_ref.dtype)


def _dw_combine(outs, lses):
    B, T, HD = outs[0].shape
    tt = min(T, 256)
    os_ = pl.BlockSpec((1, tt, HD), lambda b, t: (b, t, 0))
    ls_ = pl.BlockSpec((1, tt, C_HPG), lambda b, t: (b, t, 0))
    return pl.pallas_call(
        _dw_combine_kernel, grid=(B, T // tt), in_specs=[os_] * 3 + [ls_] * 3, out_specs=os_,
        out_shape=jax.ShapeDtypeStruct((B, T, HD), BF16),
        compiler_params=_cp("arbitrary", "arbitrary"), name="dw_combine",
    )(*outs, *lses)


def _dw_mixer(h, bufs, w):
    B, T, D = h.shape
    W3 = 3 * C_NG * C_HPG * C_DH
    proj = _linear(h.reshape(B * T, D), w['dw_w_qkv']).reshape(B, T, W3)
    HD = C_HPG * C_DH

    def group_kv(g, first_row):
        k, v = (proj[:, first_row:, (which * C_NG + g) * HD:(which * C_NG + g + 1) * HD] for which in (1, 2))
        return jnp.stack([k, v], axis=2).reshape(B, T - first_row, 2, C_HPG, C_DH)

    outs, lses, new_bufs = [], [], []
    for g, (win, dil) in enumerate(C_GROUPS):
        if bufs is None:
            o, lse = _dw_prompt_group(proj, w['rel_bias'], g)
            new_bufs.append(group_kv(g, T - min(win, T)))
        else:
            o, lse = _dw_decode_group(proj, bufs[g], w['rel_bias'], g)
            new_bufs.append(jnp.concatenate([bufs[g].astype(F32), group_kv(g, 0)], axis=1)[:, T:])
        outs.append(o)
        lses.append(lse)
    o = _dw_combine(outs, lses)
    y = _linear(o.reshape(B * T, C_HPG * C_DH), w['dw_w_out'])
    return y.reshape(B, T, D), tuple(new_bufs)


def _conv_kernel(x_ref, halo_ref, w_ref, o_ref):
    x = x_ref[0]
    halo = halo_ref[0, 0]
    tt = x.shape[0]
    head = x[:8]
    r8 = _iota2(head.shape, 0)
    acc = x * w_ref[D_CONV - 1:D_CONV, :]
    for s in range(1, D_CONV):
        top = jnp.where(r8 < s, pltpu.roll(halo, s, axis=0), pltpu.roll(head, s, axis=0))
        if tt > 8:
            shifted = jnp.concatenate([top, pltpu.roll(x, s, axis=0)[8:]], axis=0)
        else:
            shifted = top
        acc = acc + shifted * w_ref[D_CONV - 1 - s:D_CONV - s, :]
    o_ref[0] = acc * _sigmoid(acc)


def _gdn_conv(proj, conv_buf, conv_w):
    B, T, _ = proj.shape
    C = D_CONV_CH
    tt, tc = min(T, 256), 1024
    nT = T // tt
    first = jnp.pad(conv_buf.astype(F32), ((0, 0), (8 - (D_CONV - 1), 0), (0, 0)))[:, None]
    if nT > 1:
        tails = proj[:, :, :C].reshape(B, nT, tt, C)[:, :-1, tt - 8:]
        halo = jnp.concatenate([first, tails], axis=1)
    else:
        halo = first
    return pl.pallas_call(
        _conv_kernel, grid=(B, nT, C // tc),
        in_specs=[pl.BlockSpec((1, tt, tc), lambda b, t, c: (b, t, c)),
                  pl.BlockSpec((1, 1, 8, tc), lambda b, t, c: (b, t, 0, c)),
                  pl.BlockSpec((D_CONV, tc), lambda b, t, c: (0, c))],
        out_specs=pl.BlockSpec((1, tt, tc), lambda b, t, c: (b, t, c)),
        out_shape=jax.ShapeDtypeStruct((B, T, C), F32),
        compiler_params=_cp("arbitrary", "arbitrary", "arbitrary"), name="gdn_conv",
    )(proj, halo, conv_w.astype(F32))


def _gdn_kernel(par_ref, q_ref, k_ref, v_ref, z_ref, braw_ref, araw_ref, s0_ref, gn_ref, o_ref, s_out, S_s,
                *, L, nC, HG, rep):
    hg, c = pl.program_id(1), pl.program_id(2)
    DK, DV = D_DK, D_DV

    @pl.when(c == 0)
    def _():
        S_s[...] = s0_ref[0].reshape(HG * DK, DV)

    row, col = _iota2((L, L), 0), _iota2((L, L), 1)
    eye, tril = row == col, col <= row
    ident = jnp.where(eye, 1.0, 0.0)
    heads = range(HG)
    qs, ks, kks, qks = [], [], [], []
    for jq in range(HG // rep):
        q, k = q_ref[0, :, jq * DK:(jq + 1) * DK], k_ref[0, :, jq * DK:(jq + 1) * DK]
        q = q * lax.rsqrt(jnp.sum(q * q, axis=-1, keepdims=True) + EPS) * (DK ** -0.5)
        k = k * lax.rsqrt(jnp.sum(k * k, axis=-1, keepdims=True) + EPS)
        qs += [q] * rep
        ks += [k] * rep
        kks += [_dot_nt(k, k)] * rep
        qks += [_dot_nt(q, k)] * rep
    betas, Gs, GLs, decays, Ns = [], [], [], [], []
    for j in heads:
        hd = hg * HG + j
        beta_row = _sigmoid(braw_ref[j, pl.ds(c, 1), :])
        g_row = -jnp.exp(par_ref[0, hd]) * _softplus(araw_ref[j, pl.ds(c, 1), :] + par_ref[1, hd])
        beta_col, g_col = _row_to_col(beta_row, eye), _row_to_col(g_row, eye)
        G_col = jnp.sum(jnp.where(tril, g_row, 0.0), axis=1, keepdims=True)
        G_row = jnp.sum(jnp.where(row <= col, g_col, 0.0), axis=0, keepdims=True)
        decay = jnp.exp(jnp.where(tril, G_col - G_row, -jnp.inf))
        betas.append(beta_col)
        Gs.append(G_col)
        GLs.append(jnp.sum(g_row, axis=1, keepdims=True))
        decays.append(decay)
        Ns.append(jnp.where(col < row, -(beta_col * kks[j]) * decay, 0.0))
    invs = [ident + N for N in Ns]
    for _ in range(int(math.log2(L)) - 1):
        Ns = [_dot_hi(N, N) for N in Ns]
        invs = [inv + _dot_hi(inv, N) for inv, N in zip(invs, Ns)]
    eGs = [jnp.exp(G) for G in Gs]
    Us = [_dot_hi(invs[j], v_ref[0, :, j * DV:(j + 1) * DV] * betas[j]) for j in heads]
    Ws = [_dot_hi(invs[j], ks[j] * (betas[j] * eGs[j])) for j in heads]
    Ss = [S_s[j * DK:(j + 1) * DK, :] for j in heads]
    v_news = [Us[j] - _dot(Ws[j], Ss[j]) for j in heads]
    os_ = [_dot(qs[j] * eGs[j], Ss[j]) + _dot(qks[j] * decays[j], v_news[j]) for j in heads]
    states = [jnp.exp(GLs[j]) * Ss[j] + _dot_tn(ks[j] * jnp.exp(GLs[j] - Gs[j]), v_news[j]) for j in heads]
    outs = []
    for j in heads:
        o, z = os_[j], z_ref[0, :, j * DV:(j + 1) * DV]
        on = o * lax.rsqrt(jnp.mean(o * o, axis=-1, keepdims=True) + EPS) * gn_ref[...]
        outs.append((on * (z * _sigmoid(z))).astype(o_ref.dtype))
    S_new = jnp.concatenate(states, axis=0)
    S_s[...] = S_new
    o_ref[0] = jnp.concatenate(outs, axis=1)

    @pl.when(c == nC - 1)
    def _():
        s_out[0] = S_new.reshape(HG, DK, DV)


def _gdn_mixer(h, conv_buf, S0, w):
    B, T, D = h.shape
    HQ, HV, DK, DV = D_QK_HEADS, D_V_HEADS, D_DK, D_DV
    C = D_CONV_CH
    n_v = HV * DV
    n_main = C + n_v
    h2 = h.reshape(B * T, D)
    proj = _linear(h2, w['gdn_w_in'], ncols=n_main).reshape(B, T, n_main)
    w_ba = jnp.pad(w['gdn_w_in'][:, n_main:], ((0, 0), (0, LANE - 2 * HV)))
    ba = _linear(h2, w_ba)
    conv = _gdn_conv(proj, conv_buf, w['gdn_conv_w'])
    new_buf = jnp.concatenate([conv_buf.astype(F32), proj[:, :, :C]], axis=1)[:, T:] if T < D_CONV - 1 \
        else proj[:, T - (D_CONV - 1):, :C]
    L = math.gcd(T, D_CHUNK)
    nC = T // L
    to_rows = lambda a: a.reshape(B, nC, L, HV).transpose(0, 3, 1, 2).reshape(B * HV, nC, L)
    braw, araw = to_rows(ba[:, :HV]), to_rows(ba[:, HV:2 * HV])
    par = jnp.stack([w['gdn_A_log'], w['gdn_dt_bias']]).astype(F32)
    rep = HV // HQ
    HG = 8
    wq, wv = HG // rep * DK, HG * DV
    blk = lambda wd, off: pl.BlockSpec((1, L, wd), lambda b, hg, c: (b, c, off + hg))
    gs = pl.BlockSpec((HG, nC, L), lambda b, hg, c: (b * (HV // HG) + hg, 0, 0))
    ss = pl.BlockSpec((1, HG, DK, DV), lambda b, hg, c: (b, hg, 0, 0))
    o, S = pl.pallas_call(
        functools.partial(_gdn_kernel, L=L, nC=nC, HG=HG, rep=rep), grid=(B, HV // HG, nC),
        in_specs=[pl.BlockSpec(memory_space=pltpu.SMEM),
                  blk(wq, 0), blk(wq, HQ * DK // wq), blk(wv, 2 * HQ * DK // wv), blk(wv, C // wv), gs, gs, ss,
                  pl.BlockSpec((1, DV), lambda b, hg, c: (0, 0))],
        out_specs=[blk(wv, 0), ss],
        out_shape=[jax.ShapeDtypeStruct((B, T, n_v), BF16), jax.ShapeDtypeStruct((B, HV, DK, DV), F32)],
        scratch_shapes=[pltpu.VMEM((HG * DK, DV), F32)],
        compiler_params=_cp("arbitrary", "arbitrary", "arbitrary"), name="gdn",
    )(par, conv, conv, conv, proj, braw, araw, S0.astype(F32), w['gdn_norm_g'].reshape(1, DV).astype(F32))
    y = _linear(o.reshape(B * T, n_v), w['gdn_w_out'])
    return y.reshape(B, T, D), (new_buf, S)


def _ffn(h, w, i):
    B, T, D = h.shape
    act = _swiglu_up(h.reshape(B * T, D), w['ffn_w_gu'], (i,), D_FF)
    return _linear(act, w['ffn_w_down'], sel=(i,)).reshape(B, T, D)


def _moe_plan(route, n_tiles):
    E, TM = N_EXPERTS, MOE_TM
    sel1, sel2 = route[:, :E], route[:, E:2 * E]
    cnt1 = jnp.sum(sel1, axis=0)
    cnt = cnt1 + jnp.sum(sel2, axis=0)
    pcnt = jnp.ceil(cnt / TM) * TM
    pend = jnp.cumsum(pcnt)
    pstart = pend - pcnt
    rank1 = jnp.cumsum(sel1, axis=0) - sel1
    rank2 = cnt1[None] + jnp.cumsum(sel2, axis=0) - sel2
    dest1 = jnp.sum(sel1 * (pstart[None] + rank1), axis=1)
    dest2 = jnp.sum(sel2 * (pstart[None] + rank2), axis=1)
    dest = jnp.concatenate([dest1, dest2]).astype(jnp.int32)
    n_used = (pend[-1] / TM).astype(jnp.int32)
    first_row = jnp.minimum(jnp.arange(n_tiles), n_used - 1).astype(F32) * TM
    tile_expert = jnp.minimum(jnp.sum((first_row[:, None] >= pend[None, :]).astype(jnp.int32), axis=1), E - 1)
    M = route.shape[0]
    token_of = (jnp.argsort(dest) % M).astype(jnp.int32)
    of_tile = (tile_expert[:, None] == jnp.arange(E)[None, :]).astype(F32)
    per_row = lambda v: jnp.repeat(jnp.sum(of_tile * v[None], axis=1), TM)
    rank = jnp.arange(n_tiles * TM).astype(F32) - per_row(pstart)
    packed = per_row(jnp.cumsum(cnt) - cnt) + rank
    src = jnp.where(jnp.logical_and(rank >= 0, rank < per_row(cnt)),
                    token_of[jnp.clip(packed, 0, TOP_K * M - 1).astype(jnp.int32)], 0)
    return dest, src, tile_expert, n_used.reshape(1)


def _row_copy(src_hbm, src_row, dst, dst_row, sem):
    return pltpu.make_async_copy(src_hbm.at[pl.ds(src_row, 1)], dst.at[pl.ds(dst_row, 1)], sem)


def _lagged_copies(n, copies_of, lag):
    def body(t, carry):
        for cp in copies_of(t):
            cp.start()

        @pl.when(t >= lag)
        def _():
            for cp in copies_of(t - lag):
                cp.wait()
        return carry

    def drain(t, carry):
        for cp in copies_of(t):
            cp.wait()
        return carry

    lax.fori_loop(0, n, body, 0, unroll=DMA_UNROLL if n % DMA_UNROLL == 0 else 1)
    lax.fori_loop(max(n - lag, 0), n, drain, 0)


def _dispatch_kernel(src_ref, h_hbm, o_ref, buf, sem, *, TR):
    base = pl.program_id(0) * TR
    scattered = lambda r: lax.rem(r * DISPATCH_STRIDE, TR)
    _lagged_copies(TR, lambda r: [_row_copy(h_hbm, src_ref[base + scattered(r)], buf, scattered(r), sem)], DMA_LAG)
    o_ref[...] = buf[...].astype(o_ref.dtype)


def _dispatch(h, src):
    D = h.shape[1]
    P = src.shape[0]
    TR = MOE_TM
    grid_spec = pltpu.PrefetchScalarGridSpec(
        num_scalar_prefetch=1, grid=(P // TR,), in_specs=[pl.BlockSpec(memory_space=pl.ANY)],
        out_specs=pl.BlockSpec((TR, D), lambda t, src: (t, 0)),
        scratch_shapes=[pltpu.VMEM((TR, D), F32), pltpu.SemaphoreType.DMA(())])
    return pl.pallas_call(
        functools.partial(_dispatch_kernel, TR=TR), grid_spec=grid_spec,
        out_shape=jax.ShapeDtypeStruct((P, D), BF16),
        compiler_params=_cp("arbitrary"), name="moe_dispatch",
    )(src, h)


def _tile_is_new(te_ref, m):
    return jnp.logical_or(m == 0, te_ref[m] != te_ref[jnp.maximum(m - 1, 0)])


def _moe_up_kernel(te_ref, nu_ref, x_ref, wg_ref, wu_ref, o_ref, wg_bf, wu_bf):
    m = pl.program_id(1)

    @pl.when(_tile_is_new(te_ref, m))
    def _():
        wg_bf[...] = wg_ref[...].astype(BF16)
        wu_bf[...] = wu_ref[...].astype(BF16)

    @pl.when(m < nu_ref[0])
    def _():
        x = x_ref[...]
        g = jnp.dot(x, wg_bf[...], preferred_element_type=F32)
        u = jnp.dot(x, wu_bf[...], preferred_element_type=F32)
        o_ref[...] = (g * _sigmoid(g) * u).astype(o_ref.dtype)

    @pl.when(m >= nu_ref[0])
    def _():
        o_ref[...] = jnp.zeros_like(o_ref)


def _moe_down_kernel(te_ref, nu_ref, a_ref, w_ref, o_ref, w_bf):
    m = pl.program_id(1)

    @pl.when(_tile_is_new(te_ref, m))
    def _():
        w_bf[...] = w_ref[...].astype(BF16)

    @pl.when(m < nu_ref[0])
    def _():
        o_ref[...] = jnp.dot(a_ref[...], w_bf[...], preferred_element_type=F32)

    @pl.when(m >= nu_ref[0])
    def _():
        o_ref[...] = jnp.zeros_like(o_ref)


def _moe_experts(xg, tile_expert, n_used, w, i):
    P, K = xg.shape
    F, TM = D_FF_EXPERT, MOE_TM
    NT = P // TM
    tn = 512
    nb = F // tn
    row = lambda j, m, te, nu: (jnp.minimum(m, nu[0] - 1), 0)
    out = lambda j, m, te, nu: (m, j)
    act = pl.pallas_call(
        _moe_up_kernel,
        grid_spec=pltpu.PrefetchScalarGridSpec(
            num_scalar_prefetch=2, grid=(nb, NT),
            in_specs=[pl.BlockSpec((TM, K), row),
                      pl.BlockSpec((None, None, K, tn), lambda j, m, te, nu: (i, te[m], 0, j)),
                      pl.BlockSpec((None, None, K, tn), lambda j, m, te, nu: (i, te[m], 0, j + nb))],
            out_specs=pl.BlockSpec((TM, tn), out),
            scratch_shapes=[pltpu.VMEM((K, tn), BF16), pltpu.VMEM((K, tn), BF16)]),
        out_shape=jax.ShapeDtypeStruct((P, F), BF16),
        compiler_params=_cp("arbitrary", "arbitrary"), name="moe_up",
    )(tile_expert, n_used, xg, w['moe_w_gu'], w['moe_w_gu'])
    tn = _pick_tn(F, K)
    return pl.pallas_call(
        _moe_down_kernel,
        grid_spec=pltpu.PrefetchScalarGridSpec(
            num_scalar_prefetch=2, grid=(K // tn, NT),
            in_specs=[pl.BlockSpec((TM, F), row),
                      pl.BlockSpec((None, None, F, tn), lambda j, m, te, nu: (i, te[m], 0, j))],
            out_specs=pl.BlockSpec((TM, tn), out),
            scratch_shapes=[pltpu.VMEM((F, tn), BF16)]),
        out_shape=jax.ShapeDtypeStruct((P, K), F32),
        compiler_params=_cp("arbitrary", "arbitrary"), name="moe_down",
    )(tile_expert, n_used, act, w['moe_w_down'])


def _combine_kernel(pos_ref, yg_hbm, route_ref, o_ref, buf, sem, *, M, TR):
    base = pl.program_id(0) * TR
    _lagged_copies(TR, lambda r: [_row_copy(yg_hbm, pos_ref[k * M + base + r], buf.at[k], r, sem.at[k])
                                  for k in range(TOP_K)], DMA_LAG)
    g1 = route_ref[:, 2 * N_EXPERTS:2 * N_EXPERTS + 1]
    g2 = route_ref[:, 2 * N_EXPERTS + 1:2 * N_EXPERTS + 2]
    o_ref[...] = g1 * buf[0] + g2 * buf[1]


def _combine(yg, pos, route):
    M = route.shape[0]
    D = yg.shape[1]
    TR = _pick_rows(M)
    grid_spec = pltpu.PrefetchScalarGridSpec(
        num_scalar_prefetch=1, grid=(M // TR,),
        in_specs=[pl.BlockSpec(memory_space=pl.ANY), pl.BlockSpec((TR, LANE), lambda t, pos: (t, 0))],
        out_specs=pl.BlockSpec((TR, D), lambda t, pos: (t, 0)),
        scratch_shapes=[pltpu.VMEM((TOP_K, TR, D), F32), pltpu.SemaphoreType.DMA((TOP_K,))])
    return pl.pallas_call(
        functools.partial(_combine_kernel, M=M, TR=TR), grid_spec=grid_spec,
        out_shape=jax.ShapeDtypeStruct((M, D), F32),
        compiler_params=_cp("arbitrary"), name="moe_combine",
    )(pos, yg, route)


def _moe(h, w, i):
    M, D = h.shape
    n_tiles = -(-(TOP_K * M + N_EXPERTS * (MOE_TM - 1)) // MOE_TM)
    w_r = jnp.pad(w['moe_router'][i], ((0, 0), (0, LANE - N_EXPERTS)))
    route = _router(h, w_r)
    dest, src, tile_expert, n_used = _moe_plan(route, n_tiles)
    xg = _dispatch(h, src)
    yg = _moe_experts(xg, tile_expert, n_used, w, i)
    return _combine(yg, dest, route)


def _trunk(xs, mods, pasts, w):
    S = range(len(xs))
    news = [{} for _ in S]
    depth = mods[0].shape[0]
    ng = w['norm_g']
    ffn_dtype = lambda layer: BF16 if layer % 2 == 0 else F32
    hs = [_norm(xs[s], ng[0, 0], nmod=mods[s][0], mod_rows=(0, 1)) for s in S]
    outs = [None for _ in S]
    for layer in range(depth):
        kind = layer % 4
        ys = []
        for s in S:
            h, past, new = hs[s], pasts[s], news[s]
            if kind == 0:
                y, new['mlstm'] = _mlstm(h, *past['mlstm'], w)
            elif kind == 1:
                y, new['sb'] = _sb_mixer(h, past['sb'], w)
            elif kind == 2:
                y, new['dw'] = _dw_mixer(h, past['dw'], w)
            else:
                y, new['gdn'] = _gdn_mixer(h, *past['gdn'], w)
            ys.append(y)
        for s in S:
            xs[s], hs[s] = _norm(xs[s], ng[layer, 1], y=ys[s], gmod=mods[s][layer], gate_row=2,
                                 nmod=mods[s][layer], mod_rows=(3, 4), h_dtype=ffn_dtype(layer))
        if layer % 2 == 0:
            ys = [_ffn(hs[s], w, layer // 2) for s in S]
        else:
            sizes = [hs[s].shape[0] * hs[s].shape[1] for s in S]
            y_all = _moe(jnp.concatenate([hs[s].reshape(sizes[s], D_MODEL) for s in S], axis=0), w, layer // 2)
            offs = np.cumsum([0] + sizes)
            ys = [y_all[offs[s]:offs[s + 1]].reshape(hs[s].shape) for s in S]
        for s in S:
            if layer + 1 < depth:
                xs[s], hs[s] = _norm(xs[s], ng[layer + 1, 0], y=ys[s], gmod=mods[s][layer], gate_row=5,
                                     nmod=mods[s][layer + 1], mod_rows=(0, 1))
            else:
                outs[s] = _norm(xs[s], w['final_g'], y=ys[s], gmod=mods[s][layer], gate_row=5, h_dtype=F32)
    return outs, news


def kernel(x_prompt, x_sample, c_prompt, c_sample, state_mlstm_C, state_mlstm_n, state_mlstm_m, cache_kv_sb, page_table, cache_kv_dw1, cache_kv_dw2, cache_kv_dw3, state_conv_gdn, state_S_gdn, w_ada, b_ada, norm_g, final_g, mlstm_w_in, mlstm_b_if, mlstm_norm_g, mlstm_w_out, sb_w_qkv, sb_w_out, sb_bias, dw_w_qkv, dw_w_out, rel_bias, gdn_w_in, gdn_conv_w, gdn_A_log, gdn_dt_bias, gdn_norm_g, gdn_w_out, ffn_w_gu, ffn_w_down, moe_router, moe_w_gu, moe_w_down):
    w = dict(norm_g=norm_g, final_g=final_g, mlstm_w_in=mlstm_w_in, mlstm_b_if=mlstm_b_if,
             mlstm_norm_g=mlstm_norm_g, mlstm_w_out=mlstm_w_out, sb_w_qkv=sb_w_qkv, sb_w_out=sb_w_out,
             sb_bias=sb_bias, dw_w_qkv=dw_w_qkv, dw_w_out=dw_w_out, rel_bias=rel_bias, gdn_w_in=gdn_w_in,
             gdn_conv_w=gdn_conv_w, gdn_A_log=gdn_A_log, gdn_dt_bias=gdn_dt_bias, gdn_norm_g=gdn_norm_g,
             gdn_w_out=gdn_w_out, ffn_w_gu=ffn_w_gu, ffn_w_down=ffn_w_down, moe_router=moe_router,
             moe_w_gu=moe_w_gu, moe_w_down=moe_w_down)
    Bp, Bd = x_prompt.shape[0], x_sample.shape[0]
    depth = w_ada.shape[0]
    rows = -(-(Bp + Bd) // 8) * 8
    c_all = jnp.pad(jnp.concatenate([c_prompt, c_sample], axis=0), ((0, rows - Bp - Bd), (0, 0)))
    mod = _ada(c_all, w_ada, b_ada).reshape(depth, rows, 6, D_MODEL)
    past_p = {
        'mlstm': (jnp.zeros((Bp, A_HEADS, A_DK, A_DV), F32), jnp.zeros((Bp, A_HEADS, A_DK), F32),
                  jnp.zeros((Bp, A_HEADS), F32)),
        'sb': None,
        'dw': None,
        'gdn': (jnp.zeros((Bp, D_CONV - 1, D_CONV_CH), F32), jnp.zeros((Bp, D_V_HEADS, D_DK, D_DV), F32)),
    }
    past_s = {
        'mlstm': (state_mlstm_C, state_mlstm_n, state_mlstm_m),
        'sb': (cache_kv_sb, page_table),
        'dw': (cache_kv_dw1, cache_kv_dw2, cache_kv_dw3),
        'gdn': (state_conv_gdn, state_S_gdn),
    }
    (y_prompt, y_sample), (new_p, new_s) = _trunk(
        [x_prompt, x_sample], [mod[:, :Bp], mod[:, Bp:Bp + Bd]], [past_p, past_s], w)
    C_p, n_p, m_p = new_p['mlstm']
    C_s, n_s, m_s = new_s['mlstm']
    dw1_p, dw2_p, dw3_p = new_p['dw']
    dw1_s, dw2_s, dw3_s = new_s['dw']
    conv_p, S_p = new_p['gdn']
    conv_s, S_s = new_s['gdn']
    return (y_prompt, y_sample, C_p, n_p, m_p, C_s, n_s, m_s, new_p['sb'], new_s['sb'],
            dw1_p, dw2_p, dw3_p, dw1_s, dw2_s, dw3_s, conv_p, S_p, conv_s, S_s)
```

```python
import functools
import math

import jax
import jax.numpy as jnp
import numpy as np
from jax import lax
from jax.experimental import pallas as pl
from jax.experimental.pallas import tpu as pltpu

F32 = jnp.float32
BF16 = jnp.bfloat16

D_MODEL = 2048
EPS = 1e-6
A_HEADS, A_DK, A_DV, A_CHUNK = 8, 128, 256, 64
B_HEADS, B_DH = 16, 128
Q_BLOCK = 128
C_GROUPS = ((128, 1), (512, 4), (2048, 16))
C_NG, C_HPG, C_DH = 3, 8, 128
N_BUCKETS, MAX_DISTANCE = 32, 2048
D_QK_HEADS, D_V_HEADS, D_DK, D_DV, D_CONV, D_CHUNK = 16, 32, 128, 128, 4, 64
D_CONV_CH = 2 * D_QK_HEADS * D_DK + D_V_HEADS * D_DV
D_FF, N_EXPERTS, TOP_K, D_FF_EXPERT = 5632, 8, 2, 7168
MOE_TM = 512
DMA_LAG = 128
DMA_UNROLL = 4
PAGE_SIZE = 128
LANE = 128
SB_TILE = 256
SB_HEADS_PER_STEP = 4
SB_PAGE_SLOTS = 6

VMEM_LIMIT_BYTES = 56 * 1024 * 1024
WEIGHT_BLOCK_BYTES = 12 * 1024 * 1024
NEG_BIG = -1e30


def _cp(*sem):
    return pltpu.CompilerParams(dimension_semantics=sem, vmem_limit_bytes=VMEM_LIMIT_BYTES)


def _dot(a, b):
    return jnp.dot(a.astype(BF16), b.astype(BF16), preferred_element_type=F32)


def _dot_nt(a, b):
    return lax.dot_general(a.astype(BF16), b.astype(BF16), (((1,), (1,)), ((), ())),
                           preferred_element_type=F32)


def _dot_tn(a, b):
    return lax.dot_general(a.astype(BF16), b.astype(BF16), (((0,), (0,)), ((), ())),
                           preferred_element_type=F32)


def _split(a):
    hi = a.astype(BF16)
    return hi, (a - hi.astype(F32)).astype(BF16)


def _dot_hi(a, b):
    a_hi, a_lo = _split(a)
    b_hi, b_lo = _split(b)
    d = lambda x, y: jnp.dot(x, y, preferred_element_type=F32)
    return d(a_hi, b_hi) + (d(a_hi, b_lo) + d(a_lo, b_hi))


def _sigmoid(x):
    return 1.0 / (1.0 + jnp.exp(-x))


def _log_sigmoid(x):
    return jnp.minimum(x, 0.0) - jnp.log1p(jnp.exp(-jnp.abs(x)))


def _softplus(x):
    return jnp.maximum(x, 0.0) + jnp.log1p(jnp.exp(-jnp.abs(x)))


def _iota2(shape, axis):
    return lax.broadcasted_iota(jnp.int32, shape, axis)


def _row_to_col(row, eye):
    return jnp.sum(jnp.where(eye, row, 0.0), axis=1, keepdims=True)


def _pick_tn(K, N, col0=0):
    for tn in (2048, 1024, 512, 256, 128):
        if N % tn == 0 and col0 % tn == 0 and K * tn * 4 <= WEIGHT_BLOCK_BYTES:
            return tn
    raise ValueError((K, N, col0))


def _pick_tm(M, K):
    if M % 1024 == 0 and K <= 2048:
        return 1024
    return 512 if M % 512 == 0 else M


def _pick_rows(M, cap=1024):
    best = M
    for t in range(8, min(M, cap) + 1, 8):
        if M % t == 0:
            best = t
    return best


def _linear_kernel(x_ref, w_ref, o_ref, wbf_ref):
    @pl.when(pl.program_id(1) == 0)
    def _():
        wbf_ref[...] = w_ref[...].astype(BF16)

    o_ref[...] = jnp.dot(x_ref[...], wbf_ref[...], preferred_element_type=F32).astype(o_ref.dtype)


def _linear(x, w, *, sel=(), col0=0, ncols=None, out_dtype=F32):
    M, K = x.shape
    assert w.shape[-2] == K
    N = w.shape[-1] - col0 if ncols is None else ncols
    tn, tm = _pick_tn(K, N, col0), _pick_tm(M, K)
    off = col0 // tn
    w_spec = pl.BlockSpec((None,) * len(sel) + (K, tn), lambda j, m: tuple(sel) + (0, j + off))
    return pl.pallas_call(
        _linear_kernel, grid=(N // tn, M // tm),
        in_specs=[pl.BlockSpec((tm, K), lambda j, m: (m, 0)), w_spec],
        out_specs=pl.BlockSpec((tm, tn), lambda j, m: (m, j)),
        out_shape=jax.ShapeDtypeStruct((M, N), out_dtype),
        scratch_shapes=[pltpu.VMEM((K, tn), BF16)],
        compiler_params=_cp("arbitrary", "arbitrary"), name="linear",
    )(x, w)


def _gu_kernel(x_ref, wg_ref, wu_ref, o_ref, wg_bf, wu_bf):
    @pl.when(pl.program_id(1) == 0)
    def _():
        wg_bf[...] = wg_ref[...].astype(BF16)
        wu_bf[...] = wu_ref[...].astype(BF16)

    x = x_ref[...]
    g = jnp.dot(x, wg_bf[...], preferred_element_type=F32)
    u = jnp.dot(x, wu_bf[...], preferred_element_type=F32)
    o_ref[...] = (g * _sigmoid(g) * u).astype(o_ref.dtype)


def _swiglu_up(x, w, sel, F):
    M, K = x.shape
    tn, tm = 512, _pick_tm(M, K)
    lead = (None,) * len(sel)
    nb = F // tn
    return pl.pallas_call(
        _gu_kernel, grid=(nb, M // tm),
        in_specs=[pl.BlockSpec((tm, K), lambda j, m: (m, 0)),
                  pl.BlockSpec(lead + (K, tn), lambda j, m: tuple(sel) + (0, j)),
                  pl.BlockSpec(lead + (K, tn), lambda j, m: tuple(sel) + (0, j + nb))],
        out_specs=pl.BlockSpec((tm, tn), lambda j, m: (m, j)),
        out_shape=jax.ShapeDtypeStruct((M, F), BF16),
        scratch_shapes=[pltpu.VMEM((K, tn), BF16), pltpu.VMEM((K, tn), BF16)],
        compiler_params=_cp("arbitrary", "arbitrary"), name="swiglu_up",
    )(x, w, w)


def _router_kernel(x_ref, w_ref, o_ref):
    E = N_EXPERTS
    logits = jnp.dot(x_ref[...].astype(BF16), w_ref[...].astype(BF16), preferred_element_type=F32)
    lane = _iota2(logits.shape, 1)
    logits = jnp.where(lane < E, logits, -jnp.inf)
    m1 = jnp.max(logits, axis=1, keepdims=True)
    i1 = jnp.min(jnp.where(logits == m1, lane, LANE), axis=1, keepdims=True)
    rest = jnp.where(lane == i1, -jnp.inf, logits)
    m2 = jnp.max(rest, axis=1, keepdims=True)
    i2 = jnp.min(jnp.where(rest == m2, lane, LANE), axis=1, keepdims=True)
    e2 = jnp.exp(m2 - m1)
    g1 = 1.0 / (1.0 + e2)
    g2 = e2 / (1.0 + e2)
    o_ref[...] = (jnp.where(lane == i1, 1.0, 0.0) + jnp.where(lane == i2 + E, 1.0, 0.0)
                  + jnp.where(lane == 2 * E, g1, 0.0) + jnp.where(lane == 2 * E + 1, g2, 0.0))


def _router(x, w_pad):
    M, K = x.shape
    tm = _pick_rows(M)
    return pl.pallas_call(
        _router_kernel, grid=(M // tm,),
        in_specs=[pl.BlockSpec((tm, K), lambda m: (m, 0)), pl.BlockSpec((K, LANE), lambda m: (0, 0))],
        out_specs=pl.BlockSpec((tm, LANE), lambda m: (m, 0)),
        out_shape=jax.ShapeDtypeStruct((M, LANE), F32),
        compiler_params=_cp("arbitrary"), name="router",
    )(x, w_pad)


def _ada_kernel(c_ref, w_ref, b_ref, o_ref):
    c = c_ref[...]
    x = (c * _sigmoid(c)).astype(BF16)
    o_ref[0] = jnp.dot(x, w_ref[...].astype(BF16), preferred_element_type=F32) + b_ref[0]


def _ada(c_pad, w_ada, b_ada):
    R = c_pad.shape[0]
    depth, K, N = w_ada.shape
    tn = 1024
    return pl.pallas_call(
        _ada_kernel, grid=(depth, N // tn),
        in_specs=[pl.BlockSpec((R, K), lambda l, j: (0, 0)),
                  pl.BlockSpec((None, K, tn), lambda l, j: (l, 0, j)),
                  pl.BlockSpec((1, 1, tn), lambda l, j: (l, 0, j))],
        out_specs=pl.BlockSpec((1, R, tn), lambda l, j: (l, 0, j)),
        out_shape=jax.ShapeDtypeStruct((depth, R, N), F32),
        compiler_params=_cp("arbitrary", "arbitrary"), name="ada",
    )(c_pad, w_ada, b_ada.reshape(depth, 1, N))


def _norm_kernel(*refs, has_res, mod_rows, gate_row):
    refs = list(refs)
    x_ref = refs.pop(0)
    x = x_ref[0]
    if has_res:
        y_ref, gmod_ref = refs.pop(0), refs.pop(0)
        x = x + gmod_ref[0, gate_row:gate_row + 1, :] * y_ref[0]
    g_ref = refs.pop(0)
    nmod_ref = refs.pop(0) if mod_rows is not None else None
    if has_res and mod_rows is not None:
        xo_ref = refs.pop(0)
        xo_ref[0] = x
    h_ref = refs.pop(0)
    y = x * lax.rsqrt(jnp.mean(x * x, axis=-1, keepdims=True) + EPS) * g_ref[...]
    if mod_rows is not None:
        shift_row, scale_row = mod_rows
        y = y * (1.0 + nmod_ref[0, scale_row:scale_row + 1, :]) + nmod_ref[0, shift_row:shift_row + 1, :]
    h_ref[0] = y.astype(h_ref.dtype)


def _norm(x, g, *, y=None, gmod=None, gate_row=None, nmod=None, mod_rows=None, h_dtype=BF16):
    B, T, D = x.shape
    tt = min(T, 256)
    has_res = y is not None
    xs = pl.BlockSpec((1, tt, D), lambda b, t: (b, t, 0))
    ms = pl.BlockSpec((1, 6, D), lambda b, t: (b, 0, 0))
    args, specs = [x], [xs]
    if has_res:
        args += [y, gmod]
        specs += [xs, ms]
    args.append(g.reshape(1, D))
    specs.append(pl.BlockSpec((1, D), lambda b, t: (0, 0)))
    if mod_rows is not None:
        args.append(nmod)
        specs.append(ms)
    out_shape = [jax.ShapeDtypeStruct((B, T, D), h_dtype)]
    out_specs = [xs]
    if has_res and mod_rows is not None:
        out_shape.insert(0, jax.ShapeDtypeStruct((B, T, D), F32))
        out_specs.insert(0, xs)
    out = pl.pallas_call(
        functools.partial(_norm_kernel, has_res=has_res, mod_rows=mod_rows, gate_row=gate_row),
        grid=(B, T // tt), in_specs=specs, out_specs=out_specs, out_shape=out_shape,
        compiler_params=_cp("arbitrary", "arbitrary"), name="norm",
    )(*args)
    return out if len(out) > 1 else out[0]


def _mlstm_kernel(bif_ref, q_ref, k_ref, v_ref, og_ref, gi_ref, gf_ref, c0_ref, n0_ref, m0_ref, gn_ref,
                  hs_ref, c_out, n_out, m_out, C_s, n_s, m_s, *, L, nC):
    c = pl.program_id(1)
    H, DK, DV = A_HEADS, A_DK, A_DV
    heads = range(H)

    @pl.when(c == 0)
    def _():
        C_s[...] = c0_ref[0].reshape(H * DK, DV)
        n_s[...] = n0_ref[0]
        m_s[...] = m0_ref[0]

    row, col = _iota2((L, L), 0), _iota2((L, L), 1)
    eye, tril = row == col, col <= row
    qs = [q_ref[0, :, hd * DK:(hd + 1) * DK] * (DK ** -0.5) for hd in heads]
    ks = [k_ref[0, :, hd * DK:(hd + 1) * DK] for hd in heads]
    vs = [v_ref[0, :, hd * DV:(hd + 1) * DV] for hd in heads]
    Cs = [C_s[hd * DK:(hd + 1) * DK, :] for hd in heads]
    qks = [_dot_nt(qs[hd], ks[hd]) for hd in heads]
    qCs = [_dot(qs[hd], Cs[hd]) for hd in heads]
    m_ts, scs, Ds, decs, w_cols, m_news = [], [], [], [], [], []
    for hd in heads:
        li_row = gi_ref[hd, pl.ds(c, 1), :] + bif_ref[0, hd]
        lf_row = _log_sigmoid(gf_ref[hd, pl.ds(c, 1), :] + bif_ref[1, hd])
        li_col, lf_col = _row_to_col(li_row, eye), _row_to_col(lf_row, eye)
        b_col = jnp.sum(jnp.where(tril, lf_row, 0.0), axis=1, keepdims=True)
        b_row = jnp.sum(jnp.where(row <= col, lf_col, 0.0), axis=0, keepdims=True)
        b_last = jnp.sum(lf_row, axis=1, keepdims=True)
        m_prev = m_s[hd:hd + 1, :]
        dlog = jnp.where(tril, b_col - b_row + li_row, -jnp.inf)
        inter = b_col + m_prev
        m_t = jnp.maximum(inter, jnp.max(dlog, axis=1, keepdims=True))
        m_new = jnp.max(jnp.where(row[:, :1] == L - 1, m_t, -jnp.inf), axis=0, keepdims=True)
        m_ts.append(m_t)
        scs.append(jnp.exp(inter - m_t))
        Ds.append(jnp.exp(dlog - m_t))
        decs.append(jnp.exp(b_last + m_prev - m_new))
        w_cols.append(jnp.exp(b_last - b_col + li_col - m_new))
        m_news.append(m_new)
    s_qks = [qks[hd] * Ds[hd] for hd in heads]
    nums = [_dot(s_qks[hd], vs[hd]) + scs[hd] * qCs[hd] for hd in heads]
    kws = [ks[hd] * w_cols[hd] for hd in heads]
    C_new = jnp.concatenate([decs[hd] * Cs[hd] + _dot_tn(kws[hd], vs[hd]) for hd in heads], axis=0)
    row8 = _iota2((H, 1), 0)
    n_new, m_new, outs = jnp.zeros((H, DK), F32), jnp.zeros((H, 1), F32), []
    for hd in heads:
        n = n_s[hd:hd + 1, :]
        den = jnp.sum(s_qks[hd], axis=1, keepdims=True) + scs[hd] * jnp.sum(qs[hd] * n, axis=1, keepdims=True)
        hc = nums[hd] / jnp.maximum(jnp.abs(den), jnp.exp(-m_ts[hd]))
        hn = hc * lax.rsqrt(jnp.mean(hc * hc, axis=-1, keepdims=True) + EPS) * gn_ref[hd:hd + 1, :]
        outs.append((hn * _sigmoid(og_ref[0, :, hd * DV:(hd + 1) * DV])).astype(hs_ref.dtype))
        n_new = n_new + jnp.where(row8 == hd, decs[hd] * n + jnp.sum(kws[hd], axis=0, keepdims=True), 0.0)
        m_new = m_new + jnp.where(row8 == hd, m_news[hd], 0.0)
    C_s[...] = C_new
    n_s[...] = n_new
    m_s[...] = m_new
    hs_ref[0] = jnp.concatenate(outs, axis=1)

    @pl.when(c == nC - 1)
    def _():
        c_out[0] = C_new.reshape(H, DK, DV)
        n_out[0] = n_new
        m_out[0] = m_new


def _mlstm(h, C0, n0, m0, w):
    B, T, D = h.shape
    H, DK, DV = A_HEADS, A_DK, A_DV
    h2 = h.reshape(B * T, D)
    n_main = 2 * H * DK + 2 * H * DV
    proj = _linear(h2, w['mlstm_w_in'], ncols=n_main).reshape(B, T, n_main)
    w_gate = jnp.pad(w['mlstm_w_in'][:, n_main:], ((0, 0), (0, LANE - 2 * H)))
    gates = _linear(h2, w_gate)
    L = math.gcd(T, A_CHUNK)
    nC = T // L
    to_rows = lambda a: a.reshape(B, nC, L, H).transpose(0, 3, 1, 2).reshape(B * H, nC, L)
    gi, gf = to_rows(gates[:, :H]), to_rows(gates[:, H:2 * H])
    wk, wv = H * DK, H * DV
    cols = lambda wd, off: pl.BlockSpec((1, L, wd), lambda b, c: (b, c, off))
    gs = pl.BlockSpec((H, nC, L), lambda b, c: (b, 0, 0))
    cs = pl.BlockSpec((1, H, DK, DV), lambda b, c: (b, 0, 0, 0))
    ns = pl.BlockSpec((1, H, DK), lambda b, c: (b, 0, 0))
    ms = pl.BlockSpec((1, H, 1), lambda b, c: (b, 0, 0))
    hs, C, n, m = pl.pallas_call(
        functools.partial(_mlstm_kernel, L=L, nC=nC), grid=(B, nC),
        in_specs=[pl.BlockSpec(memory_space=pltpu.SMEM), cols(wk, 0), cols(wk, 1), cols(wv, 2 * wk // wv),
                  cols(wv, 2 * wk // wv + 1), gs, gs, cs, ns, ms, pl.BlockSpec((H, DV), lambda b, c: (0, 0))],
        out_specs=[cols(wv, 0), cs, ns, ms],
        out_shape=[jax.ShapeDtypeStruct((B, T, H * DV), BF16), jax.ShapeDtypeStruct((B, H, DK, DV), F32),
                   jax.ShapeDtypeStruct((B, H, DK), F32), jax.ShapeDtypeStruct((B, H, 1), F32)],
        scratch_shapes=[pltpu.VMEM((H * DK, DV), F32), pltpu.VMEM((H, DK), F32), pltpu.VMEM((H, 1), F32)],
        compiler_params=_cp("arbitrary", "arbitrary"), name="mlstm",
    )(w['mlstm_b_if'], proj, proj, proj, proj, gi, gf, C0.astype(F32), n0.astype(F32),
      m0.astype(F32).reshape(B, H, 1), w['mlstm_norm_g'].astype(F32))
    y = _linear(hs.reshape(B * T, H * DV), w['mlstm_w_out'])
    return y.reshape(B, T, D), (C, n, m.reshape(B, H))


def _sb_weights(zs, valid, laters, upper):
    lss = [_log_sigmoid(z) for z in zs]
    l1s = [ls - z for ls, z in zip(lss, zs)]
    if valid is not None:
        l1s = [jnp.where(valid, l1, 0.0) for l1 in l1s]
    his = [l1.astype(BF16) for l1 in l1s]
    los = [(l1 - hi.astype(F32)).astype(BF16) for l1, hi in zip(l1s, his)]
    afters = [jnp.dot(hi, upper, preferred_element_type=F32) + jnp.dot(lo, upper, preferred_element_type=F32)
              for hi, lo in zip(his, los)]
    as_ = [jnp.exp(ls + after + later) for ls, after, later in zip(lss, afters, laters)]
    if valid is not None:
        as_ = [jnp.where(valid, a, 0.0) for a in as_]
    return as_, [later + jnp.sum(l1, axis=1, keepdims=True) for later, l1 in zip(laters, l1s)]


def _sb_kernel(bias_ref, q_ref, k_ref, v_ref, o_ref, *, TQ, HG):
    hg, i = pl.program_id(1), pl.program_id(2)
    DH = B_DH
    heads = range(HG)
    cols = lambda hd: slice(hd * DH, (hd + 1) * DH)
    qs = [q_ref[0, :, cols(hd)] * (DH ** -0.5) for hd in heads]
    biases = [bias_ref[hg * HG + hd] for hd in heads]
    row, col = _iota2((TQ, TQ), 0), _iota2((TQ, TQ), 1)
    upper = jnp.where(row > col, 1.0, 0.0).astype(BF16)

    def tile(kk, carry, valid):
        outs, laters = carry
        keys = pl.ds(pl.multiple_of((i - kk) * TQ, TQ), TQ)
        zs = [_dot_nt(qs[hd], k_ref[0, keys, cols(hd)]) + biases[hd] for hd in heads]
        as_, laters = _sb_weights(zs, valid, laters, upper)
        return [outs[hd] + _dot(as_[hd], v_ref[0, keys, cols(hd)]) for hd in heads], laters

    init = ([jnp.zeros((TQ, DH), F32) for _ in heads], [jnp.zeros((TQ, 1), F32) for _ in heads])
    carry = tile(0, init, col < row)
    outs, _ = lax.fori_loop(1, i + 1, lambda kk, c: tile(kk, c, None), carry)
    o_ref[0] = jnp.concatenate(outs, axis=1).astype(o_ref.dtype)


def _sb_prompt(proj, bias):
    B, T, _ = proj.shape
    H, HG = B_HEADS, SB_HEADS_PER_STEP
    TQ = math.gcd(T, SB_TILE)
    wd = HG * B_DH
    kvs = lambda off: pl.BlockSpec((1, T, wd), lambda b, hg, i: (b, 0, off + hg))
    qo = pl.BlockSpec((1, TQ, wd), lambda b, hg, i: (b, i, hg))
    return pl.pallas_call(
        functools.partial(_sb_kernel, TQ=TQ, HG=HG), grid=(B, H // HG, T // TQ),
        in_specs=[pl.BlockSpec(memory_space=pltpu.SMEM), qo, kvs(H // HG), kvs(2 * H // HG)],
        out_specs=qo,
        out_shape=jax.ShapeDtypeStruct((B, T, H * B_DH), BF16),
        compiler_params=_cp("arbitrary", "arbitrary", "arbitrary"), name="sb_prompt",
    )(bias, proj, proj, proj)


def _sb_dec_kernel(pt_ref, q_ref, bias_ref, kvn_ref, cache_hbm, o_ref, kv_buf, sem, acc_s, later_s,
                   *, n_seq, n_pages, TN):
    b, p = pl.program_id(0), pl.program_id(1)
    step = b * n_pages + p
    n_steps = n_seq * n_pages
    n_slots = kv_buf.shape[0]
    ahead = n_slots - 1
    slot = lax.rem(step, n_slots)
    H = B_HEADS
    R = H * TN

    def page_copies(s, sl):
        page = pt_ref[s // n_pages, n_pages - 1 - s % n_pages]
        return [pltpu.make_async_copy(cache_hbm.at[page, :, kv, hd, :], kv_buf.at[sl, kv * H + hd], sem.at[sl])
                for kv in range(2) for hd in range(H)]

    @pl.when(step == 0)
    def _():
        for s in range(min(ahead, n_steps)):
            for cp in page_copies(s, s % n_slots):
                cp.start()

    @pl.when(step + ahead < n_steps)
    def _():
        for cp in page_copies(step + ahead, lax.rem(step + ahead, n_slots)):
            cp.start()

    bias = bias_ref[...]
    row, col = _iota2((PAGE_SIZE, PAGE_SIZE), 0), _iota2((PAGE_SIZE, PAGE_SIZE), 1)
    upper = jnp.where(row > col, 1.0, 0.0).astype(BF16)

    def segment(rows_of, valid, later):
        z = jnp.concatenate([_dot_nt(q_ref[0, hd], rows_of(hd)) for hd in range(H)], axis=0) + bias
        (a,), (later,) = _sb_weights([z], valid, [later], upper)
        out = jnp.concatenate([_dot(a[hd * TN:(hd + 1) * TN], rows_of(H + hd)) for hd in range(H)], axis=0)
        return out, later

    @pl.when(p == 0)
    def _():
        rq, ck = _iota2((R, PAGE_SIZE), 0), _iota2((R, PAGE_SIZE), 1)
        out, later = segment(lambda i: kvn_ref[0, i], ck < lax.rem(rq, TN), jnp.zeros((R, 1), F32))
        acc_s[...] = out
        later_s[...] = later

    for cp in page_copies(step, slot):
        cp.wait()
    out, later = segment(lambda i: kv_buf[slot, i], None, later_s[...])
    acc_s[...] += out
    later_s[...] = later

    @pl.when(p == n_pages - 1)
    def _():
        o_ref[0] = acc_s[...]


def _sb_decode(proj, cache, page_table, bias):
    B, TN, _ = proj.shape
    H, DH = B_HEADS, B_DH
    HD = H * DH
    R = H * TN
    n_pages = page_table.shape[1]
    q = proj[:, :, :HD].reshape(B, TN, H, DH).transpose(0, 2, 1, 3) * (DH ** -0.5)
    kvn = proj[:, :, HD:].reshape(B, TN, 2 * H, DH).transpose(0, 2, 1, 3)
    kvn = jnp.pad(kvn, ((0, 0), (0, 0), (0, PAGE_SIZE - TN), (0, 0)))
    bias_col = jnp.repeat(bias.astype(F32), TN).reshape(R, 1)
    grid_spec = pltpu.PrefetchScalarGridSpec(
        num_scalar_prefetch=1, grid=(B, n_pages),
        in_specs=[pl.BlockSpec((1, H, TN, DH), lambda b, p, pt: (b, 0, 0, 0)),
                  pl.BlockSpec((R, 1), lambda b, p, pt: (0, 0)),
                  pl.BlockSpec((1, 2 * H, PAGE_SIZE, DH), lambda b, p, pt: (b, 0, 0, 0)),
                  pl.BlockSpec(memory_space=pl.ANY)],
        out_specs=pl.BlockSpec((1, R, DH), lambda b, p, pt: (b, 0, 0)),
        scratch_shapes=[pltpu.VMEM((SB_PAGE_SLOTS, 2 * H, PAGE_SIZE, DH), F32),
                        pltpu.SemaphoreType.DMA((SB_PAGE_SLOTS,)),
                        pltpu.VMEM((R, DH), F32), pltpu.VMEM((R, 1), F32)])
    o = pl.pallas_call(
        functools.partial(_sb_dec_kernel, n_seq=B, n_pages=n_pages, TN=TN), grid_spec=grid_spec,
        out_shape=jax.ShapeDtypeStruct((B, R, DH), F32),
        compiler_params=_cp("arbitrary", "arbitrary"), name="sb_decode",
    )(page_table, q, bias_col, kvn, cache)
    return o.reshape(B, H, TN, DH).transpose(0, 2, 1, 3).reshape(B, TN, HD).astype(BF16)


def _sb_mixer(h, past, w):
    B, T, D = h.shape
    proj = _linear(h.reshape(B * T, D), w['sb_w_qkv']).reshape(B, T, 3 * B_HEADS * B_DH)
    if past is None:
        o = _sb_prompt(proj, w['sb_bias'])
    else:
        o = _sb_decode(proj, past[0], past[1], w['sb_bias'])
    y = _linear(o.reshape(B * T, B_HEADS * B_DH), w['sb_w_out'])
    kv_new = proj[:, :, B_HEADS * B_DH:].reshape(B, T, 2, B_HEADS, B_DH)
    return y.reshape(B, T, D), kv_new


def _t5_bucket(dist):
    max_exact = N_BUCKETS // 2
    large = max_exact + (jnp.log(jnp.maximum(dist, 1).astype(F32) / max_exact)
                         / math.log(MAX_DISTANCE / max_exact) * (N_BUCKETS - max_exact)).astype(jnp.int32)
    return jnp.where(dist < max_exact, dist, jnp.minimum(large, N_BUCKETS - 1))


def _tap_bias(rel_bias, g, taps, valid):
    win, dil = C_GROUPS[g]
    J = win // dil + 1
    valid = jnp.logical_and(valid, jnp.logical_and(taps >= 0, taps < J))
    tab = rel_bias[_t5_bucket(dil * jnp.arange(J))][:, g * C_HPG:(g + 1) * C_HPG].astype(F32)
    hit = taps[None, ..., None] == jnp.arange(J)
    vals = jnp.sum(jnp.where(hit, tab.T.reshape((C_HPG,) + (1,) * taps.ndim + (J,)), 0.0), axis=-1)
    return jnp.where(valid[None], vals, -jnp.inf)


def _band_kernel(q_ref, kp_ref, kc_ref, vp_ref, vc_ref, bias_ref, o_ref, lse_ref, *, TQ):
    i = pl.program_id(2)
    H, DH = C_HPG, C_DH
    heads = range(H)
    head = lambda ref, hd: ref[0, :, hd * DH:(hd + 1) * DH]
    qs = [head(q_ref, hd) * (DH ** -0.5) for hd in heads]
    s_cs = [_dot_nt(qs[hd], head(kc_ref, hd)) + bias_ref[hd, :, TQ:] for hd in heads]
    s_ps = [jnp.where(i > 0, _dot_nt(qs[hd], head(kp_ref, hd)) + bias_ref[hd, :, :TQ], -jnp.inf) for hd in heads]
    mxs = [jnp.maximum(jnp.max(s_cs[hd], axis=1, keepdims=True), jnp.max(s_ps[hd], axis=1, keepdims=True))
           for hd in heads]
    p_cs = [jnp.exp(s_cs[hd] - mxs[hd]) for hd in heads]
    p_ps = [jnp.exp(s_ps[hd] - mxs[hd]) for hd in heads]
    ls = [jnp.sum(p_cs[hd], axis=1, keepdims=True) + jnp.sum(p_ps[hd], axis=1, keepdims=True) for hd in heads]
    outs = [(_dot(p_cs[hd], head(vc_ref, hd)) + _dot(p_ps[hd], head(vp_ref, hd))) / ls[hd] for hd in heads]
    o_ref[0] = jnp.concatenate(outs, axis=1)
    lane = _iota2((TQ, H), 1)
    lse = jnp.zeros((TQ, H), F32)
    for hd in heads:
        lse = lse + jnp.where(lane == hd, mxs[hd] + jnp.log(ls[hd]), 0.0)
    lse_ref[0, 0] = lse


def _dw_prompt_group(proj, rel_bias, g):
    B, T, W3 = proj.shape
    win, dil = C_GROUPS[g]
    H, DH = C_HPG, C_DH
    TQ = win // dil
    Ts = T // dil
    assert Ts % TQ == 0
    HD = H * DH
    nb = 3
    cols = [proj[:, :, (which * C_NG + g) * HD:(which * C_NG + g + 1) * HD] for which in range(nb)]
    pv = jnp.concatenate(cols, axis=2).reshape(B, Ts, dil * nb * HD)
    t_loc, s_loc = jnp.arange(TQ)[:, None], jnp.arange(2 * TQ)[None, :] - TQ
    bias = _tap_bias(rel_bias, g, t_loc - s_loc, jnp.ones((TQ, 2 * TQ), bool))
    blk = lambda which, prev: pl.BlockSpec(
        (1, TQ, HD), lambda b, r, i: (b, jnp.maximum(i - 1, 0) if prev else i, r * nb + which))
    o, lse = pl.pallas_call(
        functools.partial(_band_kernel, TQ=TQ), grid=(B, dil, Ts // TQ),
        in_specs=[blk(0, False), blk(1, True), blk(1, False), blk(2, True), blk(2, False),
                  pl.BlockSpec((H, TQ, 2 * TQ), lambda b, r, i: (0, 0, 0))],
        out_specs=[pl.BlockSpec((1, TQ, HD), lambda b, r, i: (b, i, r)),
                   pl.BlockSpec((1, 1, TQ, H), lambda b, r, i: (b, r, i, 0))],
        out_shape=[jax.ShapeDtypeStruct((B, Ts, dil * HD), F32),
                   jax.ShapeDtypeStruct((B, dil, Ts, H), F32)],
        compiler_params=_cp("arbitrary", "arbitrary", "arbitrary"), name="dw_band",
    )(pv, pv, pv, pv, pv, bias)
    lse = lse.transpose(0, 2, 1, 3).reshape(B, T, H)
    return o.reshape(B, T, H * DH), lse


def _dw_dec_kernel(q_ref, bmn_ref, bm_ref, kvn_ref, buf_ref, o_ref, lse_ref, m_s, l_s, acc_s, *, n_tiles, TN):
    wi = pl.program_id(1)
    H = C_HPG
    R = H * TN

    def segment(src_ref, bm, m_old, l_old, acc_old):
        n_keys = bm.shape[1]
        head_rows = lambda first: src_ref[pl.ds(0, 1), pl.ds(first, n_keys, stride=2 * H), :][0]
        s = jnp.concatenate([_dot_nt(q_ref[0, hd], head_rows(hd)) for hd in range(H)], axis=0) + bm
        m_new = jnp.maximum(m_old, jnp.max(s, axis=1, keepdims=True))
        alpha = jnp.exp(m_old - m_new)
        p = jnp.exp(s - m_new)
        pv = jnp.concatenate([_dot(p[hd * TN:(hd + 1) * TN], head_rows(H + hd)) for hd in range(H)], axis=0)
        m_s[...] = m_new
        l_s[...] = alpha * l_old + jnp.sum(p, axis=1, keepdims=True)
        acc_s[...] = alpha * acc_old + pv

    @pl.when(wi == 0)
    def _():
        segment(kvn_ref, bmn_ref[...], jnp.full((R, 1), NEG_BIG, F32), jnp.zeros((R, 1), F32),
                jnp.zeros((R, C_DH), F32))

    segment(buf_ref, bm_ref[...], m_s[...], l_s[...], acc_s[...])

    @pl.when(wi == n_tiles - 1)
    def _():
        l = l_s[...]
        lse_ref[0] = m_s[...] + jnp.log(l)
        o_ref[0] = acc_s[...] / l


def _dw_decode_group(proj, buf, rel_bias, g):
    B, TN, _ = proj.shape
    win, dil = C_GROUPS[g]
    H, DH = C_HPG, C_DH
    HD = H * DH
    R = H * TN
    W = buf.shape[1]
    TW = min(W, 512)
    p6 = proj.reshape(B, TN, 3, C_NG, H, DH)
    q = p6[:, :, 0, g].transpose(0, 2, 1, 3) * (DH ** -0.5)
    kvn = jnp.pad(p6[:, :, 1:, g].reshape(B, TN * 2 * H, DH), ((0, 0), (0, (LANE - TN) * 2 * H), (0, 0)))
    t = jnp.arange(TN)[:, None]
    dist_buf = W + t - jnp.arange(W)[None, :]
    dist_new = t - jnp.arange(LANE)[None, :]
    bias_of = lambda dist, ok: _tap_bias(rel_bias, g, dist // dil, jnp.logical_and(ok, dist % dil == 0))
    bm = bias_of(dist_buf, jnp.ones_like(dist_buf, bool)).reshape(R, W)
    bmn = bias_of(dist_new, jnp.arange(LANE)[None, :] < TN).reshape(R, LANE)
    o, lse = pl.pallas_call(
        functools.partial(_dw_dec_kernel, n_tiles=W // TW, TN=TN), grid=(B, W // TW),
        in_specs=[pl.BlockSpec((1, H, TN, DH), lambda b, wi: (b, 0, 0, 0)),
                  pl.BlockSpec((R, LANE), lambda b, wi: (0, 0)),
                  pl.BlockSpec((R, TW), lambda b, wi: (0, wi)),
                  pl.BlockSpec((1, LANE * 2 * H, DH), lambda b, wi: (b, 0, 0)),
                  pl.BlockSpec((1, TW * 2 * H, DH), lambda b, wi: (b, wi, 0))],
        out_specs=[pl.BlockSpec((1, R, DH), lambda b, wi: (b, 0, 0)),
                   pl.BlockSpec((1, R, 1), lambda b, wi: (b, 0, 0))],
        out_shape=[jax.ShapeDtypeStruct((B, R, DH), F32), jax.ShapeDtypeStruct((B, R, 1), F32)],
        scratch_shapes=[pltpu.VMEM((R, 1), F32), pltpu.VMEM((R, 1), F32), pltpu.VMEM((R, DH), F32)],
        compiler_params=_cp("arbitrary", "arbitrary"), name="dw_decode",
    )(q, bmn, bm, kvn, buf.reshape(B, W * 2 * H, DH))
    o = o.reshape(B, H, TN, DH).transpose(0, 2, 1, 3).reshape(B, TN, HD)
    lse = lse.reshape(B, H, TN).transpose(0, 2, 1)
    return o, lse


def _dw_combine_kernel(o0, o1, o2, l0, l1, l2, out_ref):
    ls = [l0[0], l1[0], l2[0]]
    mx = jnp.maximum(jnp.maximum(ls[0], ls[1]), ls[2])
    es = [jnp.exp(l - mx) for l in ls]
    tot = es[0] + es[1] + es[2]
    ws = [e / tot for e in es]
    for hd in range(C_HPG):
        cols = slice(hd * C_DH, (hd + 1) * C_DH)
        acc = ws[0][:, hd:hd + 1] * o0[0, :, cols]
        acc = acc + ws[1][:, hd:hd + 1] * o1[0, :, cols]
        acc = acc + ws[2][:, hd:hd + 1] * o2[0, :, cols]
        out_ref[0, :, cols] = acc.astype(out_ref.dtype)


def _dw_combine(outs, lses):
    B, T, HD = outs[0].shape
    tt = min(T, 256)
    os_ = pl.BlockSpec((1, tt, HD), lambda b, t: (b, t, 0))
    ls_ = pl.BlockSpec((1, tt, C_HPG), lambda b, t: (b, t, 0))
    return pl.pallas_call(
        _dw_combine_kernel, grid=(B, T // tt), in_specs=[os_] * 3 + [ls_] * 3, out_specs=os_,
        out_shape=jax.ShapeDtypeStruct((B, T, HD), BF16),
        compiler_params=_cp("arbitrary", "arbitrary"), name="dw_combine",
    )(*outs, *lses)


def _dw_mixer(h, bufs, w):
    B, T, D = h.shape
    W3 = 3 * C_NG * C_HPG * C_DH
    proj = _linear(h.reshape(B * T, D), w['dw_w_qkv']).reshape(B, T, W3)
    HD = C_HPG * C_DH

    def group_kv(g, first_row):
        k, v = (proj[:, first_row:, (which * C_NG + g) * HD:(which * C_NG + g + 1) * HD] for which in (1, 2))
        return jnp.stack([k, v], axis=2).reshape(B, T - first_row, 2, C_HPG, C_DH)

    outs, lses, new_bufs = [], [], []
    for g, (win, dil) in enumerate(C_GROUPS):
        if bufs is None:
            o, lse = _dw_prompt_group(proj, w['rel_bias'], g)
            new_bufs.append(group_kv(g, T - min(win, T)))
        else:
            o, lse = _dw_decode_group(proj, bufs[g], w['rel_bias'], g)
            new_bufs.append(jnp.concatenate([bufs[g].astype(F32), group_kv(g, 0)], axis=1)[:, T:])
        outs.append(o)
        lses.append(lse)
    o = _dw_combine(outs, lses)
    y = _linear(o.reshape(B * T, C_HPG * C_DH), w['dw_w_out'])
    return y.reshape(B, T, D), tuple(new_bufs)


def _conv_kernel(x_ref, halo_ref, w_ref, o_ref):
    x = x_ref[0]
    halo = halo_ref[0, 0]
    tt = x.shape[0]
    head = x[:8]
    r8 = _iota2(head.shape, 0)
    acc = x * w_ref[D_CONV - 1:D_CONV, :]
    for s in range(1, D_CONV):
        top = jnp.where(r8 < s, pltpu.roll(halo, s, axis=0), pltpu.roll(head, s, axis=0))
        if tt > 8:
            shifted = jnp.concatenate([top, pltpu.roll(x, s, axis=0)[8:]], axis=0)
        else:
            shifted = top
        acc = acc + shifted * w_ref[D_CONV - 1 - s:D_CONV - s, :]
    o_ref[0] = acc * _sigmoid(acc)


def _gdn_conv(proj, conv_buf, conv_w):
    B, T, _ = proj.shape
    C = D_CONV_CH
    tt, tc = min(T, 256), 1024
    nT = T // tt
    first = jnp.pad(conv_buf.astype(F32), ((0, 0), (8 - (D_CONV - 1), 0), (0, 0)))[:, None]
    if nT > 1:
        tails = proj[:, :, :C].reshape(B, nT, tt, C)[:, :-1, tt - 8:]
        halo = jnp.concatenate([first, tails], axis=1)
    else:
        halo = first
    return pl.pallas_call(
        _conv_kernel, grid=(B, nT, C // tc),
        in_specs=[pl.BlockSpec((1, tt, tc), lambda b, t, c: (b, t, c)),
                  pl.BlockSpec((1, 1, 8, tc), lambda b, t, c: (b, t, 0, c)),
                  pl.BlockSpec((D_CONV, tc), lambda b, t, c: (0, c))],
        out_specs=pl.BlockSpec((1, tt, tc), lambda b, t, c: (b, t, c)),
        out_shape=jax.ShapeDtypeStruct((B, T, C), F32),
        compiler_params=_cp("arbitrary", "arbitrary", "arbitrary"), name="gdn_conv",
    )(proj, halo, conv_w.astype(F32))


def _gdn_kernel(par_ref, q_ref, k_ref, v_ref, z_ref, braw_ref, araw_ref, s0_ref, gn_ref, o_ref, s_out, S_s,
                *, L, nC, HG, rep):
    hg, c = pl.program_id(1), pl.program_id(2)
    DK, DV = D_DK, D_DV

    @pl.when(c == 0)
    def _():
        S_s[...] = s0_ref[0].reshape(HG * DK, DV)

    row, col = _iota2((L, L), 0), _iota2((L, L), 1)
    eye, tril = row == col, col <= row
    ident = jnp.where(eye, 1.0, 0.0)
    heads = range(HG)
    qs, ks, kks, qks = [], [], [], []
    for jq in range(HG // rep):
        q, k = q_ref[0, :, jq * DK:(jq + 1) * DK], k_ref[0, :, jq * DK:(jq + 1) * DK]
        q = q * lax.rsqrt(jnp.sum(q * q, axis=-1, keepdims=True) + EPS) * (DK ** -0.5)
        k = k * lax.rsqrt(jnp.sum(k * k, axis=-1, keepdims=True) + EPS)
        qs += [q] * rep
        ks += [k] * rep
        kks += [_dot_nt(k, k)] * rep
        qks += [_dot_nt(q, k)] * rep
    betas, Gs, GLs, decays, Ns = [], [], [], [], []
    for j in heads:
        hd = hg * HG + j
        beta_row = _sigmoid(braw_ref[j, pl.ds(c, 1), :])
        g_row = -jnp.exp(par_ref[0, hd]) * _softplus(araw_ref[j, pl.ds(c, 1), :] + par_ref[1, hd])
        beta_col, g_col = _row_to_col(beta_row, eye), _row_to_col(g_row, eye)
        G_col = jnp.sum(jnp.where(tril, g_row, 0.0), axis=1, keepdims=True)
        G_row = jnp.sum(jnp.where(row <= col, g_col, 0.0), axis=0, keepdims=True)
        decay = jnp.exp(jnp.where(tril, G_col - G_row, -jnp.inf))
        betas.append(beta_col)
        Gs.append(G_col)
        GLs.append(jnp.sum(g_row, axis=1, keepdims=True))
        decays.append(decay)
        Ns.append(jnp.where(col < row, -(beta_col * kks[j]) * decay, 0.0))
    invs = [ident + N for N in Ns]
    for _ in range(int(math.log2(L)) - 1):
        Ns = [_dot_hi(N, N) for N in Ns]
        invs = [inv + _dot_hi(inv, N) for inv, N in zip(invs, Ns)]
    eGs = [jnp.exp(G) for G in Gs]
    Us = [_dot_hi(invs[j], v_ref[0, :, j * DV:(j + 1) * DV] * betas[j]) for j in heads]
    Ws = [_dot_hi(invs[j], ks[j] * (betas[j] * eGs[j])) for j in heads]
    Ss = [S_s[j * DK:(j + 1) * DK, :] for j in heads]
    v_news = [Us[j] - _dot(Ws[j], Ss[j]) for j in heads]
    os_ = [_dot(qs[j] * eGs[j], Ss[j]) + _dot(qks[j] * decays[j], v_news[j]) for j in heads]
    states = [jnp.exp(GLs[j]) * Ss[j] + _dot_tn(ks[j] * jnp.exp(GLs[j] - Gs[j]), v_news[j]) for j in heads]
    outs = []
    for j in heads:
        o, z = os_[j], z_ref[0, :, j * DV:(j + 1) * DV]
        on = o * lax.rsqrt(jnp.mean(o * o, axis=-1, keepdims=True) + EPS) * gn_ref[...]
        outs.append((on * (z * _sigmoid(z))).astype(o_ref.dtype))
    S_new = jnp.concatenate(states, axis=0)
    S_s[...] = S_new
    o_ref[0] = jnp.concatenate(outs, axis=1)

    @pl.when(c == nC - 1)
    def _():
        s_out[0] = S_new.reshape(HG, DK, DV)


def _gdn_mixer(h, conv_buf, S0, w):
    B, T, D = h.shape
    HQ, HV, DK, DV = D_QK_HEADS, D_V_HEADS, D_DK, D_DV
    C = D_CONV_CH
    n_v = HV * DV
    n_main = C + n_v
    h2 = h.reshape(B * T, D)
    proj = _linear(h2, w['gdn_w_in'], ncols=n_main).reshape(B, T, n_main)
    w_ba = jnp.pad(w['gdn_w_in'][:, n_main:], ((0, 0), (0, LANE - 2 * HV)))
    ba = _linear(h2, w_ba)
    conv = _gdn_conv(proj, conv_buf, w['gdn_conv_w'])
    new_buf = jnp.concatenate([conv_buf.astype(F32), proj[:, :, :C]], axis=1)[:, T:] if T < D_CONV - 1 \
        else proj[:, T - (D_CONV - 1):, :C]
    L = math.gcd(T, D_CHUNK)
    nC = T // L
    to_rows = lambda a: a.reshape(B, nC, L, HV).transpose(0, 3, 1, 2).reshape(B * HV, nC, L)
    braw, araw = to_rows(ba[:, :HV]), to_rows(ba[:, HV:2 * HV])
    par = jnp.stack([w['gdn_A_log'], w['gdn_dt_bias']]).astype(F32)
    rep = HV // HQ
    HG = 8
    wq, wv = HG // rep * DK, HG * DV
    blk = lambda wd, off: pl.BlockSpec((1, L, wd), lambda b, hg, c: (b, c, off + hg))
    gs = pl.BlockSpec((HG, nC, L), lambda b, hg, c: (b * (HV // HG) + hg, 0, 0))
    ss = pl.BlockSpec((1, HG, DK, DV), lambda b, hg, c: (b, hg, 0, 0))
    o, S = pl.pallas_call(
        functools.partial(_gdn_kernel, L=L, nC=nC, HG=HG, rep=rep), grid=(B, HV // HG, nC),
        in_specs=[pl.BlockSpec(memory_space=pltpu.SMEM),
                  blk(wq, 0), blk(wq, HQ * DK // wq), blk(wv, 2 * HQ * DK // wv), blk(wv, C // wv), gs, gs, ss,
                  pl.BlockSpec((1, DV), lambda b, hg, c: (0, 0))],
        out_specs=[blk(wv, 0), ss],
        out_shape=[jax.ShapeDtypeStruct((B, T, n_v), BF16), jax.ShapeDtypeStruct((B, HV, DK, DV), F32)],
        scratch_shapes=[pltpu.VMEM((HG * DK, DV), F32)],
        compiler_params=_cp("arbitrary", "arbitrary", "arbitrary"), name="gdn",
    )(par, conv, conv, conv, proj, braw, araw, S0.astype(F32), w['gdn_norm_g'].reshape(1, DV).astype(F32))
    y = _linear(o.reshape(B * T, n_v), w['gdn_w_out'])
    return y.reshape(B, T, D), (new_buf, S)


def _ffn(h, w, i):
    B, T, D = h.shape
    act = _swiglu_up(h.reshape(B * T, D), w['ffn_w_gu'], (i,), D_FF)
    return _linear(act, w['ffn_w_down'], sel=(i,)).reshape(B, T, D)


def _moe_plan(route, n_tiles):
    E, TM = N_EXPERTS, MOE_TM
    sel1, sel2 = route[:, :E], route[:, E:2 * E]
    cnt1 = jnp.sum(sel1, axis=0)
    cnt = cnt1 + jnp.sum(sel2, axis=0)
    pcnt = jnp.ceil(cnt / TM) * TM
    pend = jnp.cumsum(pcnt)
    pstart = pend - pcnt
    rank1 = jnp.cumsum(sel1, axis=0) - sel1
    rank2 = cnt1[None] + jnp.cumsum(sel2, axis=0) - sel2
    dest1 = jnp.sum(sel1 * (pstart[None] + rank1), axis=1)
    dest2 = jnp.sum(sel2 * (pstart[None] + rank2), axis=1)
    dest = jnp.concatenate([dest1, dest2]).astype(jnp.int32)
    n_used = (pend[-1] / TM).astype(jnp.int32)
    first_row = jnp.minimum(jnp.arange(n_tiles), n_used - 1).astype(F32) * TM
    tile_expert = jnp.minimum(jnp.sum((first_row[:, None] >= pend[None, :]).astype(jnp.int32), axis=1), E - 1)
    M = route.shape[0]
    token_of = (jnp.argsort(dest) % M).astype(jnp.int32)
    of_tile = (tile_expert[:, None] == jnp.arange(E)[None, :]).astype(F32)
    per_row = lambda v: jnp.repeat(jnp.sum(of_tile * v[None], axis=1), TM)
    rank = jnp.arange(n_tiles * TM).astype(F32) - per_row(pstart)
    packed = per_row(jnp.cumsum(cnt) - cnt) + rank
    src = jnp.where(jnp.logical_and(rank >= 0, rank < per_row(cnt)),
                    token_of[jnp.clip(packed, 0, TOP_K * M - 1).astype(jnp.int32)], 0)
    return dest, src, tile_expert, n_used.reshape(1)


def _row_copy(src_hbm, src_row, dst, dst_row, sem):
    return pltpu.make_async_copy(src_hbm.at[pl.ds(src_row, 1)], dst.at[pl.ds(dst_row, 1)], sem)


def _lagged_copies(n, copies_of, lag):
    def body(t, carry):
        for cp in copies_of(t):
            cp.start()

        @pl.when(t >= lag)
        def _():
            for cp in copies_of(t - lag):
                cp.wait()
        return carry

    def drain(t, carry):
        for cp in copies_of(t):
            cp.wait()
        return carry

    lax.fori_loop(0, n, body, 0, unroll=DMA_UNROLL if n % DMA_UNROLL == 0 else 1)
    lax.fori_loop(max(n - lag, 0), n, drain, 0)


def _dispatch_kernel(src_ref, h_hbm, o_ref, buf, sem, *, TR):
    base = pl.program_id(0) * TR
    _lagged_copies(TR, lambda r: [_row_copy(h_hbm, src_ref[base + r], buf, r, sem)], DMA_LAG)
    o_ref[...] = buf[...].astype(o_ref.dtype)


def _dispatch(h, src):
    D = h.shape[1]
    P = src.shape[0]
    TR = MOE_TM
    grid_spec = pltpu.PrefetchScalarGridSpec(
        num_scalar_prefetch=1, grid=(P // TR,), in_specs=[pl.BlockSpec(memory_space=pl.ANY)],
        out_specs=pl.BlockSpec((TR, D), lambda t, src: (t, 0)),
        scratch_shapes=[pltpu.VMEM((TR, D), F32), pltpu.SemaphoreType.DMA(())])
    return pl.pallas_call(
        functools.partial(_dispatch_kernel, TR=TR), grid_spec=grid_spec,
        out_shape=jax.ShapeDtypeStruct((P, D), BF16),
        compiler_params=_cp("arbitrary"), name="moe_dispatch",
    )(src, h)


def _tile_is_new(te_ref, m):
    return jnp.logical_or(m == 0, te_ref[m] != te_ref[jnp.maximum(m - 1, 0)])


def _moe_up_kernel(te_ref, nu_ref, x_ref, wg_ref, wu_ref, o_ref, wg_bf, wu_bf):
    m = pl.program_id(1)

    @pl.when(_tile_is_new(te_ref, m))
    def _():
        wg_bf[...] = wg_ref[...].astype(BF16)
        wu_bf[...] = wu_ref[...].astype(BF16)

    @pl.when(m < nu_ref[0])
    def _():
        x = x_ref[...]
        g = jnp.dot(x, wg_bf[...], preferred_element_type=F32)
        u = jnp.dot(x, wu_bf[...], preferred_element_type=F32)
        o_ref[...] = (g * _sigmoid(g) * u).astype(o_ref.dtype)

    @pl.when(m >= nu_ref[0])
    def _():
        o_ref[...] = jnp.zeros_like(o_ref)


def _moe_down_kernel(te_ref, nu_ref, a_ref, w_ref, o_ref, w_bf):
    m = pl.program_id(1)

    @pl.when(_tile_is_new(te_ref, m))
    def _():
        w_bf[...] = w_ref[...].astype(BF16)

    @pl.when(m < nu_ref[0])
    def _():
        o_ref[...] = jnp.dot(a_ref[...], w_bf[...], preferred_element_type=F32)

    @pl.when(m >= nu_ref[0])
    def _():
        o_ref[...] = jnp.zeros_like(o_ref)


def _moe_experts(xg, tile_expert, n_used, w, i):
    P, K = xg.shape
    F, TM = D_FF_EXPERT, MOE_TM
    NT = P // TM
    tn = 512
    nb = F // tn
    row = lambda j, m, te, nu: (jnp.minimum(m, nu[0] - 1), 0)
    out = lambda j, m, te, nu: (m, j)
    act = pl.pallas_call(
        _moe_up_kernel,
        grid_spec=pltpu.PrefetchScalarGridSpec(
            num_scalar_prefetch=2, grid=(nb, NT),
            in_specs=[pl.BlockSpec((TM, K), row),
                      pl.BlockSpec((None, None, K, tn), lambda j, m, te, nu: (i, te[m], 0, j)),
                      pl.BlockSpec((None, None, K, tn), lambda j, m, te, nu: (i, te[m], 0, j + nb))],
            out_specs=pl.BlockSpec((TM, tn), out),
            scratch_shapes=[pltpu.VMEM((K, tn), BF16), pltpu.VMEM((K, tn), BF16)]),
        out_shape=jax.ShapeDtypeStruct((P, F), BF16),
        compiler_params=_cp("arbitrary", "arbitrary"), name="moe_up",
    )(tile_expert, n_used, xg, w['moe_w_gu'], w['moe_w_gu'])
    tn = _pick_tn(F, K)
    return pl.pallas_call(
        _moe_down_kernel,
        grid_spec=pltpu.PrefetchScalarGridSpec(
            num_scalar_prefetch=2, grid=(K // tn, NT),
            in_specs=[pl.BlockSpec((TM, F), row),
                      pl.BlockSpec((None, None, F, tn), lambda j, m, te, nu: (i, te[m], 0, j))],
            out_specs=pl.BlockSpec((TM, tn), out),
            scratch_shapes=[pltpu.VMEM((F, tn), BF16)]),
        out_shape=jax.ShapeDtypeStruct((P, K), F32),
        compiler_params=_cp("arbitrary", "arbitrary"), name="moe_down",
    )(tile_expert, n_used, act, w['moe_w_down'])


def _combine_kernel(pos_ref, yg_hbm, route_ref, o_ref, buf, sem, *, M, TR):
    base = pl.program_id(0) * TR
    _lagged_copies(TR, lambda r: [_row_copy(yg_hbm, pos_ref[k * M + base + r], buf.at[k], r, sem.at[k])
                                  for k in range(TOP_K)], DMA_LAG)
    g1 = route_ref[:, 2 * N_EXPERTS:2 * N_EXPERTS + 1]
    g2 = route_ref[:, 2 * N_EXPERTS + 1:2 * N_EXPERTS + 2]
    o_ref[...] = g1 * buf[0] + g2 * buf[1]


def _combine(yg, pos, route):
    M = route.shape[0]
    D = yg.shape[1]
    TR = _pick_rows(M)
    grid_spec = pltpu.PrefetchScalarGridSpec(
        num_scalar_prefetch=1, grid=(M // TR,),
        in_specs=[pl.BlockSpec(memory_space=pl.ANY), pl.BlockSpec((TR, LANE), lambda t, pos: (t, 0))],
        out_specs=pl.BlockSpec((TR, D), lambda t, pos: (t, 0)),
        scratch_shapes=[pltpu.VMEM((TOP_K, TR, D), F32), pltpu.SemaphoreType.DMA((TOP_K,))])
    return pl.pallas_call(
        functools.partial(_combine_kernel, M=M, TR=TR), grid_spec=grid_spec,
        out_shape=jax.ShapeDtypeStruct((M, D), F32),
        compiler_params=_cp("arbitrary"), name="moe_combine",
    )(pos, yg, route)


def _moe(h, w, i):
    M, D = h.shape
    n_tiles = -(-(TOP_K * M + N_EXPERTS * (MOE_TM - 1)) // MOE_TM)
    w_r = jnp.pad(w['moe_router'][i], ((0, 0), (0, LANE - N_EXPERTS)))
    route = _router(h, w_r)
    dest, src, tile_expert, n_used = _moe_plan(route, n_tiles)
    xg = _dispatch(h, src)
    yg = _moe_experts(xg, tile_expert, n_used, w, i)
    return _combine(yg, dest, route)


def _trunk(xs, mods, pasts, w):
    S = range(len(xs))
    news = [{} for _ in S]
    depth = mods[0].shape[0]
    ng = w['norm_g']
    ffn_dtype = lambda layer: BF16 if layer % 2 == 0 else F32
    hs = [_norm(xs[s], ng[0, 0], nmod=mods[s][0], mod_rows=(0, 1)) for s in S]
    outs = [None for _ in S]
    for layer in range(depth):
        kind = layer % 4
        ys = []
        for s in S:
            h, past, new = hs[s], pasts[s], news[s]
            if kind == 0:
                y, new['mlstm'] = _mlstm(h, *past['mlstm'], w)
            elif kind == 1:
                y, new['sb'] = _sb_mixer(h, past['sb'], w)
            elif kind == 2:
                y, new['dw'] = _dw_mixer(h, past['dw'], w)
            else:
                y, new['gdn'] = _gdn_mixer(h, *past['gdn'], w)
            ys.append(y)
        for s in S:
            xs[s], hs[s] = _norm(xs[s], ng[layer, 1], y=ys[s], gmod=mods[s][layer], gate_row=2,
                                 nmod=mods[s][layer], mod_rows=(3, 4), h_dtype=ffn_dtype(layer))
        if layer % 2 == 0:
            ys = [_ffn(hs[s], w, layer // 2) for s in S]
        else:
            sizes = [hs[s].shape[0] * hs[s].shape[1] for s in S]
            y_all = _moe(jnp.concatenate([hs[s].reshape(sizes[s], D_MODEL) for s in S], axis=0), w, layer // 2)
            offs = np.cumsum([0] + sizes)
            ys = [y_all[offs[s]:offs[s + 1]].reshape(hs[s].shape) for s in S]
        for s in S:
            if layer + 1 < depth:
                xs[s], hs[s] = _norm(xs[s], ng[layer + 1, 0], y=ys[s], gmod=mods[s][layer], gate_row=5,
                                     nmod=mods[s][layer + 1], mod_rows=(0, 1))
            else:
                outs[s] = _norm(xs[s], w['final_g'], y=ys[s], gmod=mods[s][layer], gate_row=5, h_dtype=F32)
    return outs, news


def kernel(x_prompt, x_sample, c_prompt, c_sample, state_mlstm_C, state_mlstm_n, state_mlstm_m, cache_kv_sb, page_table, cache_kv_dw1, cache_kv_dw2, cache_kv_dw3, state_conv_gdn, state_S_gdn, w_ada, b_ada, norm_g, final_g, mlstm_w_in, mlstm_b_if, mlstm_norm_g, mlstm_w_out, sb_w_qkv, sb_w_out, sb_bias, dw_w_qkv, dw_w_out, rel_bias, gdn_w_in, gdn_conv_w, gdn_A_log, gdn_dt_bias, gdn_norm_g, gdn_w_out, ffn_w_gu, ffn_w_down, moe_router, moe_w_gu, moe_w_down):
    w = dict(norm_g=norm_g, final_g=final_g, mlstm_w_in=mlstm_w_in, mlstm_b_if=mlstm_b_if,
             mlstm_norm_g=mlstm_norm_g, mlstm_w_out=mlstm_w_out, sb_w_qkv=sb_w_qkv, sb_w_out=sb_w_out,
             sb_bias=sb_bias, dw_w_qkv=dw_w_qkv, dw_w_out=dw_w_out, rel_bias=rel_bias, gdn_w_in=gdn_w_in,
             gdn_conv_w=gdn_conv_w, gdn_A_log=gdn_A_log, gdn_dt_bias=gdn_dt_bias, gdn_norm_g=gdn_norm_g,
             gdn_w_out=gdn_w_out, ffn_w_gu=ffn_w_gu, ffn_w_down=ffn_w_down, moe_router=moe_router,
             moe_w_gu=moe_w_gu, moe_w_down=moe_w_down)
    Bp, Bd = x_prompt.shape[0], x_sample.shape[0]
    depth = w_ada.shape[0]
    rows = -(-(Bp + Bd) // 8) * 8
    c_all = jnp.pad(jnp.concatenate([c_prompt, c_sample], axis=0), ((0, rows - Bp - Bd), (0, 0)))
    mod = _ada(c_all, w_ada, b_ada).reshape(depth, rows, 6, D_MODEL)
    past_p = {
        'mlstm': (jnp.zeros((Bp, A_HEADS, A_DK, A_DV), F32), jnp.zeros((Bp, A_HEADS, A_DK), F32),
                  jnp.zeros((Bp, A_HEADS), F32)),
        'sb': None,
        'dw': None,
        'gdn': (jnp.zeros((Bp, D_CONV - 1, D_CONV_CH), F32), jnp.zeros((Bp, D_V_HEADS, D_DK, D_DV), F32)),
    }
    past_s = {
        'mlstm': (state_mlstm_C, state_mlstm_n, state_mlstm_m),
        'sb': (cache_kv_sb, page_table),
        'dw': (cache_kv_dw1, cache_kv_dw2, cache_kv_dw3),
        'gdn': (state_conv_gdn, state_S_gdn),
    }
    (y_prompt, y_sample), (new_p, new_s) = _trunk(
        [x_prompt, x_sample], [mod[:, :Bp], mod[:, Bp:Bp + Bd]], [past_p, past_s], w)
    C_p, n_p, m_p = new_p['mlstm']
    C_s, n_s, m_s = new_s['mlstm']
    dw1_p, dw2_p, dw3_p = new_p['dw']
    dw1_s, dw2_s, dw3_s = new_s['dw']
    conv_p, S_p = new_p['gdn']
    conv_s, S_s = new_s['gdn']
    return (y_prompt, y_sample, C_p, n_p, m_p, C_s, n_s, m_s, new_p['sb'], new_s['sb'],
            dw1_p, dw2_p, dw3_p, dw1_s, dw2_s, dw3_s, conv_p, S_p, conv_s, S_s)
```
